```python
import math
import jax, jax.numpy as jnp
from jax import lax
import numpy as np

D_MODEL = 2048
BATCH = 2
SEQ = 8192
DEPTH = 2
DEC_BATCH = 8
DEC_SEQ = 2048
PAST_LEN = 128

N_HEADS = 8
N_KV_HEADS = 2
HEAD_DIM = 128
Q_GROUP = N_HEADS // N_KV_HEADS
ATTN_WIDTH = N_HEADS * HEAD_DIM
KV_WIDTH = N_KV_HEADS * HEAD_DIM
WINDOW = 128
BLOCK = 128
N_BUCKETS = 32
MAX_DISTANCE = 128
CONV_WIDTH = D_MODEL // 4
CONV_K = 3
LRU_WIDTH = D_MODEL // 4
LRU_HEADS = 4
LRU_BLOCK = LRU_WIDTH // LRU_HEADS
LRU_CONV_K = 4
LRU_C = 8.0
MIX_WIDTH = ATTN_WIDTH + CONV_WIDTH + LRU_WIDTH
IN_SPLITS = (ATTN_WIDTH, KV_WIDTH, KV_WIDTH, ATTN_WIDTH,
             CONV_WIDTH, CONV_WIDTH, CONV_WIDTH, CONV_WIDTH,
             LRU_WIDTH, LRU_WIDTH)
IN_WIDTH = sum(IN_SPLITS)
PLE_DIM = 256
NORM_EPS = 1e-6
NEG_INF = -1e30

kernel_name = 'hybrid_bidir_parallel_heads_encoder'


def _rmsnorm(x, g):
    x32 = x.astype(jnp.float32)
    y = x32 * lax.rsqrt(jnp.mean(x32 * x32, axis=-1, keepdims=True) + NORM_EPS)
    return (y * g.astype(jnp.float32)).astype(x.dtype)


def _t5_bucket(rel):
    half = N_BUCKETS // 2
    max_exact = half // 2
    ret = jnp.where(rel > 0, half, 0).astype(jnp.int32)
    n = jnp.abs(rel)
    nf = jnp.maximum(n, 1).astype(jnp.float32)
    large = max_exact + (jnp.log(nf / max_exact) / math.log(MAX_DISTANCE / max_exact)
                         * (half - max_exact)).astype(jnp.int32)
    large = jnp.minimum(large, half - 1)
    return ret + jnp.where(n < max_exact, n, large)


def _depthwise_conv(x, w, pad):
    c = x.shape[-1]
    return lax.conv_general_dilated(
        x, w[:, None, :].astype(x.dtype), window_strides=(1,), padding=[pad],
        dimension_numbers=('NWC', 'WIO', 'NWC'), feature_group_count=c)


def _windowed_gqa(q, k, v, rel_table, sink):
    b, s = q.shape[0], q.shape[1]
    nb = s // BLOCK
    qb = q.reshape(b, nb, BLOCK, N_KV_HEADS, Q_GROUP, HEAD_DIM)

    def band(t):
        tp = jnp.pad(t, ((0, 0), (BLOCK, BLOCK), (0, 0)))
        tp = tp.reshape(b, nb + 2, BLOCK, N_KV_HEADS, HEAD_DIM)
        return jnp.concatenate([tp[:, :-2], tp[:, 1:-1], tp[:, 2:]], axis=2)

    kb, vb = band(k), band(v)
    scores = jnp.einsum('bnqkgd,bnckd->bnkgqc', qb, kb).astype(jnp.float32) * (HEAD_DIM ** -0.5)
    q_off = jnp.arange(BLOCK, dtype=jnp.int32)[:, None]
    c_off = jnp.arange(3 * BLOCK, dtype=jnp.int32)[None, :]
    rel = c_off - BLOCK - q_off
    bias = rel_table.astype(jnp.float32)[_t5_bucket(rel)]
    bias = jnp.transpose(bias, (2, 0, 1)).reshape(N_KV_HEADS, Q_GROUP, BLOCK, 3 * BLOCK)
    key_pos = jnp.arange(nb, dtype=jnp.int32)[:, None] * BLOCK - BLOCK + c_off
    valid = (jnp.abs(rel) <= WINDOW)[None] & ((key_pos >= 0) & (key_pos < s))[:, None, :]
    scores = jnp.where(valid[None, :, None, None], scores + bias, NEG_INF)
    sink32 = sink.astype(jnp.float32).reshape(N_KV_HEADS, Q_GROUP, 1, 1)
    m = jnp.maximum(jnp.max(scores, axis=-1, keepdims=True), sink32)
    e = jnp.exp(scores - m)
    den = jnp.sum(e, axis=-1, keepdims=True) + jnp.exp(sink32 - m)
    probs = (e / den).astype(v.dtype)
    out = jnp.einsum('bnkgqc,bnckd->bnqkgd', probs, vb)
    return out.reshape(b, s, ATTN_WIDTH)


def _lin_combine(left, right):
    a_l, b_l = left
    a_r, b_r = right
    return a_l * a_r, a_r * b_l + b_r


def _rglru_direction(x, conv_w, conv_b, w_a, b_a, w_i, b_i, lam, pad, reverse):
    b, s = x.shape[0], x.shape[1]
    xc = _depthwise_conv(x, conv_w, pad) + conv_b.astype(x.dtype)
    xh = xc.reshape(b, s, LRU_HEADS, LRU_BLOCK)
    r = jax.nn.sigmoid(jnp.einsum('bshi,hij->bshj', xh, w_a).reshape(b, s, LRU_WIDTH).astype(jnp.float32)
                       + b_a.astype(jnp.float32))
    i = jax.nn.sigmoid(jnp.einsum('bshi,hij->bshj', xh, w_i).reshape(b, s, LRU_WIDTH).astype(jnp.float32)
                       + b_i.astype(jnp.float32))
    log_a = -LRU_C * r * jax.nn.softplus(-lam.astype(jnp.float32))
    a = jnp.exp(log_a)
    u = jnp.sqrt(-jnp.expm1(2.0 * log_a)) * (i * xc.astype(jnp.float32))
    _, hs = lax.associative_scan(_lin_combine, (a, u), reverse=reverse, axis=1)
    return hs


def _layer(h, p, norm_g, w_in, w_out, rel_table, sink, conv_w, lru_conv_w, lru_conv_b,
           lru_w_a, lru_b_a, lru_w_i, lru_b_i, lru_L, ple_norm, ple_w_gate, ple_w_proj):
    u = _rmsnorm(h, norm_g)
    z = jnp.einsum('bsd,de->bse', u, w_in)
    idx = np.cumsum(IN_SPLITS)[:-1].tolist()
    q, k, v, g_attn, c_b, c_c, c_x, g_conv, x_lru, g_lru = jnp.split(z, idx, axis=-1)
    y_attn = _windowed_gqa(q, k, v, rel_table, sink) * jax.nn.silu(g_attn)
    y_conv = c_b * _depthwise_conv(c_c * c_x, conv_w, (1, 1)) * jax.nn.silu(g_conv)
    y_fwd = _rglru_direction(x_lru, lru_conv_w[0], lru_conv_b[0], lru_w_a[0], lru_b_a[0],
                             lru_w_i[0], lru_b_i[0], lru_L[0], (LRU_CONV_K - 1, 0), False)
    y_bwd = _rglru_direction(x_lru, lru_conv_w[1], lru_conv_b[1], lru_w_a[1], lru_b_a[1],
                             lru_w_i[1], lru_b_i[1], lru_L[1], (0, LRU_CONV_K - 1), True)
    y_lru = (y_fwd + y_bwd).astype(h.dtype) * jax.nn.silu(g_lru)
    mix = jnp.concatenate([y_attn, y_conv, y_lru], axis=-1)
    h = h + jnp.einsum('bse,ed->bsd', mix, w_out)
    gate = jax.nn.sigmoid(jnp.einsum('bsd,de->bse', _rmsnorm(h, ple_norm), ple_w_gate))
    return h + gate * jnp.einsum('bsp,pd->bsd', p, ple_w_proj)


def _trunk(x, p, norm_mix, w_in, w_out, rel_bias, attn_sink, conv_w, lru_conv_w, lru_conv_b,
           lru_w_a, lru_b_a, lru_w_i, lru_b_i, lru_L, ple_norm, ple_w_gate, ple_w_proj, final_norm):
    h = x
    for l in range(DEPTH):
        h = _layer(h, p[l], norm_mix[l], w_in[l], w_out[l], rel_bias, attn_sink[l], conv_w[l],
                   lru_conv_w[l], lru_conv_b[l], lru_w_a[l], lru_b_a[l], lru_w_i[l], lru_b_i[l],
                   lru_L[l], ple_norm[l], ple_w_gate[l], ple_w_proj[l])
    return _rmsnorm(h, final_norm)


def setup_inputs(seed: int = 0) -> dict:
    key = jax.random.key(seed)
    ks = jax.random.split(key, 24)
    f32 = jnp.float32

    def nrm(k, shape, scale):
        return scale * jax.random.normal(k, shape, f32)

    a0 = jax.random.uniform(ks[16], (DEPTH, 2, LRU_WIDTH), f32, 0.9, 0.999)
    s0 = a0 ** (1.0 / LRU_C)
    lru_L = jnp.log(s0) - jnp.log1p(-s0)
    return {
        'x_prompt': nrm(ks[0], (BATCH, SEQ, D_MODEL), 1.0),
        'x_sample': nrm(ks[1], (DEC_BATCH, DEC_SEQ, D_MODEL), 1.0),
        'p_prompt': nrm(ks[2], (DEPTH, BATCH, SEQ, PLE_DIM), 1.0),
        'p_sample': nrm(ks[3], (DEPTH, DEC_BATCH, DEC_SEQ, PLE_DIM), 1.0),
        'norm_mix': 1.0 + nrm(ks[4], (DEPTH, D_MODEL), 0.02),
        'w_in': nrm(ks[5], (DEPTH, D_MODEL, IN_WIDTH), D_MODEL ** -0.5),
        'w_out': nrm(ks[6], (DEPTH, MIX_WIDTH, D_MODEL), MIX_WIDTH ** -0.5),
        'rel_bias': nrm(ks[7], (N_BUCKETS, N_HEADS), 0.5),
        'attn_sink': nrm(ks[8], (DEPTH, N_HEADS), 0.5),
        'conv_w': nrm(ks[9], (DEPTH, CONV_K, CONV_WIDTH), CONV_K ** -0.5),
        'lru_conv_w': nrm(ks[10], (DEPTH, 2, LRU_CONV_K, LRU_WIDTH), LRU_CONV_K ** -0.5),
        'lru_conv_b': nrm(ks[11], (DEPTH, 2, LRU_WIDTH), 0.02),
        'lru_w_a': nrm(ks[12], (DEPTH, 2, LRU_HEADS, LRU_BLOCK, LRU_BLOCK), LRU_BLOCK ** -0.5),
        'lru_b_a': nrm(ks[13], (DEPTH, 2, LRU_WIDTH), 0.1),
        'lru_w_i': nrm(ks[14], (DEPTH, 2, LRU_HEADS, LRU_BLOCK, LRU_BLOCK), LRU_BLOCK ** -0.5),
        'lru_b_i': nrm(ks[15], (DEPTH, 2, LRU_WIDTH), 0.1),
        'lru_L': lru_L,
        'ple_norm': 1.0 + nrm(ks[17], (DEPTH, D_MODEL), 0.02),
        'ple_w_gate': nrm(ks[18], (DEPTH, D_MODEL, D_MODEL), D_MODEL ** -0.5),
        'ple_w_proj': nrm(ks[19], (DEPTH, PLE_DIM, D_MODEL), PLE_DIM ** -0.5),
        'final_norm': 1.0 + nrm(ks[20], (D_MODEL,), 0.02),
    }


def reference(x_prompt, x_sample, p_prompt, p_sample, norm_mix, w_in, w_out, rel_bias, attn_sink,
              conv_w, lru_conv_w, lru_conv_b, lru_w_a, lru_b_a, lru_w_i, lru_b_i, lru_L,
              ple_norm, ple_w_gate, ple_w_proj, final_norm):
    y_prompt = _trunk(x_prompt, p_prompt, norm_mix, w_in, w_out, rel_bias, attn_sink, conv_w,
                      lru_conv_w, lru_conv_b, lru_w_a, lru_b_a, lru_w_i, lru_b_i, lru_L,
                      ple_norm, ple_w_gate, ple_w_proj, final_norm)
    y_sample = _trunk(x_sample, p_sample, norm_mix, w_in, w_out, rel_bias, attn_sink, conv_w,
                      lru_conv_w, lru_conv_b, lru_w_a, lru_b_a, lru_w_i, lru_b_i, lru_L,
                      ple_norm, ple_w_gate, ple_w_proj, final_norm)
    return (y_prompt, y_sample)
```

```python
import functools
import math

import numpy as np
import jax
import jax.numpy as jnp
from jax import lax
from jax.experimental import pallas as pl
from jax.experimental.pallas import tpu as pltpu

F32 = jnp.float32
BF16 = jnp.bfloat16

D_MODEL = 2048
N_HEADS = 8
N_KV_HEADS = 2
Q_GROUP = N_HEADS // N_KV_HEADS
HEAD_DIM = 128
ATTN_WIDTH = N_HEADS * HEAD_DIM
KV_WIDTH = N_KV_HEADS * HEAD_DIM
WINDOW = 128
BLOCK = 128
N_BUCKETS = 32
MAX_DISTANCE = 128
CONV_WIDTH = 512
CONV_K = 3
LRU_WIDTH = 512
LRU_HEADS = 4
LRU_BLOCK = 128
LRU_CONV_K = 4
LRU_C = 8.0
MIX_WIDTH = ATTN_WIDTH + CONV_WIDTH + LRU_WIDTH
IN_WIDTH = 2 * ATTN_WIDTH + 2 * KV_WIDTH + 4 * CONV_WIDTH + 2 * LRU_WIDTH
PLE_DIM = 256
NORM_EPS = 1e-6
NEG_INF = -1e30

COL = 512
COL_KV = 2
COL_G_ATTN = 3
COL_CONV_B = 5
COL_CONV_C = 6
COL_CONV_X = 7
COL_G_CONV = 8
COL_LRU_X = 9
COL_G_LRU = 10

SUBLANES = 8
VMEM_LIMIT = 56 * 1024 * 1024

INPROJ_TM = 1024
INPROJ_TN = 512
ATTN_QB = 2
MIX_T = 256
OUT_TM = 256


def _sigmoid(x):
    return 1.0 / (1.0 + jnp.exp(-x))


def _silu(x):
    return x * _sigmoid(x)


def _rms(x, g):
    ms = jnp.mean(x * x, axis=-1, keepdims=True)
    return x * lax.rsqrt(ms + NORM_EPS) * g


def _inproj_kernel(h_ref, g_ref, w_ref, z_ref, u_ref):
    @pl.when(pl.program_id(1) == 0)
    def _():
        u_ref[...] = _rms(h_ref[...], g_ref[...]).astype(BF16)

    z_ref[...] = jnp.dot(u_ref[...], w_ref[...], preferred_element_type=F32).astype(z_ref.dtype)


def _inproj(h2d, g, w_b):
    m = h2d.shape[0]
    tm, tn = INPROJ_TM, INPROJ_TN
    return pl.pallas_call(
        _inproj_kernel,
        grid=(m // tm, IN_WIDTH // tn),
        in_specs=[
            pl.BlockSpec((tm, D_MODEL), lambda i, j: (i, 0)),
            pl.BlockSpec((1, D_MODEL), lambda i, j: (0, 0)),
            pl.BlockSpec((D_MODEL, tn), lambda i, j: (0, j)),
        ],
        out_specs=pl.BlockSpec((tm, tn), lambda i, j: (i, j)),
        out_shape=jax.ShapeDtypeStruct((m, IN_WIDTH), BF16),
        scratch_shapes=[pltpu.VMEM((tm, D_MODEL), BF16)],
        compiler_params=pltpu.CompilerParams(
            dimension_semantics=("arbitrary", "arbitrary"), vmem_limit_bytes=VMEM_LIMIT),
        name="inproj",
    )(h2d, g, w_b)


def _bucket_table():
    q = np.arange(BLOCK)[:, None]
    c = np.arange(3 * BLOCK)[None, :]
    rel = c - BLOCK - q
    n = np.abs(rel)
    half = N_BUCKETS // 2
    max_exact = half // 2
    n2 = np.maximum(n, 1).astype(np.int64) ** 2
    floor_log2 = np.floor(np.log2(n2.astype(np.float64)) + 1e-9).astype(np.int64)
    large = np.minimum(max_exact + floor_log2 - 6, half - 1)
    bucket = np.where(rel > 0, half, 0) + np.where(n < max_exact, n, large)
    bucket = np.where(n <= WINDOW, bucket, -1)
    return bucket.astype(np.int32)


def _bias_kernel(tab_ref, bkt_ref, o_ref):
    h = pl.program_id(0)
    b = bkt_ref[...]
    acc = jnp.full(b.shape, NEG_INF, F32)
    for k in range(N_BUCKETS):
        acc = jnp.where(b == k, tab_ref[k, h], acc)
    o_ref[0] = acc


def _rel_bias(rel_table):
    bkt = jnp.asarray(_bucket_table())
    return pl.pallas_call(
        _bias_kernel,
        grid=(N_HEADS,),
        in_specs=[
            pl.BlockSpec(memory_space=pltpu.SMEM),
            pl.BlockSpec((BLOCK, 3 * BLOCK), lambda h: (0, 0)),
        ],
        out_specs=pl.BlockSpec((1, BLOCK, 3 * BLOCK), lambda h: (h, 0, 0)),
        out_shape=jax.ShapeDtypeStruct((N_HEADS, BLOCK, 3 * BLOCK), F32),
        name="relbias",
    )(rel_table, bkt)


def _attn_kernel(sink_ref, q_ref, kvp_ref, kvc_ref, kvn_ref, g0_ref, g1_ref, bias_ref, o_ref, *, nblk):
    i = pl.program_id(1)
    scale = HEAD_DIM ** -0.5
    col = lax.broadcasted_iota(jnp.int32, (1, 3 * BLOCK), 1)
    g_refs = (g0_ref, g1_ref)
    for kh in range(N_KV_HEADS):
        ks = slice(kh * HEAD_DIM, (kh + 1) * HEAD_DIM)
        vs = slice(KV_WIDTH + kh * HEAD_DIM, KV_WIDTH + (kh + 1) * HEAD_DIM)
        k_win = jnp.concatenate([kvp_ref[:, ks], kvc_ref[:, ks], kvn_ref[:, ks]], axis=0)
        v_win = jnp.concatenate([kvp_ref[:, vs], kvc_ref[:, vs], kvn_ref[:, vs]], axis=0)
        bias4 = bias_ref[kh * Q_GROUP:(kh + 1) * Q_GROUP].reshape(Q_GROUP * BLOCK, 3 * BLOCK)
        for s in range(ATTN_QB):
            n = i * ATTN_QB + s
            rows = slice(s * BLOCK, (s + 1) * BLOCK)
            q4 = jnp.concatenate(
                [q_ref[rows, (kh * Q_GROUP + g) * HEAD_DIM:(kh * Q_GROUP + g + 1) * HEAD_DIM]
                 for g in range(Q_GROUP)], axis=0)
            k_sub = k_win[s * BLOCK:s * BLOCK + 3 * BLOCK]
            v_sub = v_win[s * BLOCK:s * BLOCK + 3 * BLOCK]
            sc = lax.dot_general(q4, k_sub, (((1,), (1,)), ((), ())), preferred_element_type=F32)
            sc = sc * scale + bias4
            outside = ((col < BLOCK) & (n == 0)) | ((col >= 2 * BLOCK) & (n == nblk - 1))
            sc = jnp.where(outside, NEG_INF, sc)
            es, dens = [], []
            for g in range(Q_GROUP):
                sg = sc[g * BLOCK:(g + 1) * BLOCK]
                sink = sink_ref[kh * Q_GROUP + g]
                m = jnp.maximum(jnp.max(sg, axis=-1, keepdims=True), sink)
                e = jnp.exp(sg - m)
                dens.append(jnp.sum(e, axis=-1, keepdims=True) + jnp.exp(sink - m))
                es.append(e.astype(BF16))
            pv = jnp.dot(jnp.concatenate(es, axis=0), v_sub, preferred_element_type=F32)
            for g in range(Q_GROUP):
                hd = kh * Q_GROUP + g
                o = pv[g * BLOCK:(g + 1) * BLOCK] / dens[g]
                gcols = slice((hd % 4) * HEAD_DIM, (hd % 4 + 1) * HEAD_DIM)
                gate = g_refs[hd // 4][rows, gcols].astype(F32)
                o_ref[rows, hd * HEAD_DIM:(hd + 1) * HEAD_DIM] = (o * _silu(gate)).astype(o_ref.dtype)


def _attention(z3, bias, sink):
    b, s, _ = z3.shape
    nblk = s // BLOCK
    tq = ATTN_QB * BLOCK
    kern = functools.partial(_attn_kernel, nblk=nblk)
    return pl.pallas_call(
        kern,
        grid=(b, s // tq),
        in_specs=[
            pl.BlockSpec(memory_space=pltpu.SMEM),
            pl.BlockSpec((None, tq, ATTN_WIDTH), lambda bi, i: (bi, i, 0)),
            pl.BlockSpec((None, BLOCK, COL), lambda bi, i: (bi, jnp.maximum(i * ATTN_QB - 1, 0), COL_KV)),
            pl.BlockSpec((None, tq, COL), lambda bi, i: (bi, i, COL_KV)),
            pl.BlockSpec((None, BLOCK, COL),
                         lambda bi, i: (bi, jnp.minimum((i + 1) * ATTN_QB, nblk - 1), COL_KV)),
            pl.BlockSpec((None, tq, COL), lambda bi, i: (bi, i, COL_G_ATTN)),
            pl.BlockSpec((None, tq, COL), lambda bi, i: (bi, i, COL_G_ATTN + 1)),
            pl.BlockSpec((N_HEADS, BLOCK, 3 * BLOCK), lambda bi, i: (0, 0, 0)),
        ],
        out_specs=pl.BlockSpec((None, tq, ATTN_WIDTH), lambda bi, i: (bi, i, 0)),
        out_shape=jax.ShapeDtypeStruct((b, s, ATTN_WIDTH), BF16),
        compiler_params=pltpu.CompilerParams(
            dimension_semantics=("arbitrary", "arbitrary"), vmem_limit_bytes=VMEM_LIMIT),
        name="attn",
    )(sink, z3, z3, z3, z3, z3, z3, bias)


def _local_scan(a, u, reverse):
    row = lax.broadcasted_iota(jnp.int32, (1, SUBLANES, a.shape[-1]), 1)
    d = 1
    while d < SUBLANES:
        shift = SUBLANES - d if reverse else d
        a_sh = pltpu.roll(a, shift, axis=1)
        u_sh = pltpu.roll(u, shift, axis=1)
        keep = (row < SUBLANES - d) if reverse else (row >= d)
        u = jnp.where(keep, a * u_sh + u, u)
        a = jnp.where(keep, a * a_sh, a)
        d *= 2
    return a, u


def _chain(a, u, h, reverse):
    groups = a.shape[0]
    out = [None] * groups
    order = range(groups - 1, -1, -1) if reverse else range(groups)
    edge = 0 if reverse else SUBLANES - 1
    for g in order:
        hg = a[g] * h + u[g]
        out[g] = hg
        h = hg[edge:edge + 1, :]
    return jnp.stack(out, axis=0), h


def _lru_gates(xc, wai_ref, ba_ref, bi_ref, lam_ref, d):
    ra, ri = [], []
    for hd in range(LRU_HEADS):
        xh = xc[:, hd * LRU_BLOCK:(hd + 1) * LRU_BLOCK].astype(BF16)
        y = jnp.dot(xh, wai_ref[d, hd], preferred_element_type=F32)
        ra.append(y[:, :LRU_BLOCK])
        ri.append(y[:, LRU_BLOCK:])
    r = _sigmoid(jnp.concatenate(ra, axis=1) + ba_ref[d:d + 1, :])
    gi = _sigmoid(jnp.concatenate(ri, axis=1) + bi_ref[d:d + 1, :])
    x = -lam_ref[d:d + 1, :]
    softplus = jnp.maximum(x, 0.0) + jnp.log1p(jnp.exp(-jnp.abs(x)))
    log_a = (-LRU_C * softplus) * r
    a = jnp.exp(log_a)
    u = jnp.sqrt(-jnp.tanh(log_a) * (a * a + 1.0)) * (gi * xc)
    return a, u


def _mixers_kernel(cb_ref, cc_ref, cx_ref, gc_ref, xf_ref, xb_ref, ccn_ref, cxn_ref,
                   cw_ref, lw_ref, lb_ref, wai_ref, ba_ref, bi_ref, lam_ref,
                   yc_ref, hf_ref, hb_ref,
                   bufc, bufx, bufy, carry_f, carry_b, *, nt):
    i = pl.program_id(1)
    t = MIX_T
    groups = t // SUBLANES
    pad = SUBLANES

    @pl.when(i == 0)
    def _():
        zeros = jnp.zeros((pad, LRU_WIDTH), F32)
        bufc[0:pad, :] = zeros
        bufx[0:pad, :] = zeros
        bufy[t:t + pad, :] = zeros
        carry_f[...] = zeros
        carry_b[...] = zeros

    bufc[pad:pad + t, :] = cc_ref[...].astype(F32) * cx_ref[...].astype(F32)
    nxt = ccn_ref[0:pad, :].astype(F32) * cxn_ref[0:pad, :].astype(F32)
    bufc[pad + t:pad + t + pad, :] = jnp.where(i == nt - 1, 0.0, nxt)
    conv = (cw_ref[0:1, :] * bufc[pad - 1:pad - 1 + t, :]
            + cw_ref[1:2, :] * bufc[pad:pad + t, :]
            + cw_ref[2:3, :] * bufc[pad + 1:pad + 1 + t, :])
    yc_ref[...] = (cb_ref[...].astype(F32) * conv * _silu(gc_ref[...].astype(F32))).astype(yc_ref.dtype)
    bufc[0:pad, :] = bufc[t:t + pad, :]

    bufx[pad:pad + t, :] = xf_ref[...].astype(F32)
    xc = lb_ref[0:1, :] + sum(
        lw_ref[0, k:k + 1, :] * bufx[pad - (LRU_CONV_K - 1) + k:pad - (LRU_CONV_K - 1) + k + t, :]
        for k in range(LRU_CONV_K))
    a, u = _lru_gates(xc, wai_ref, ba_ref, bi_ref, lam_ref, 0)
    a, u = _local_scan(a.reshape(groups, SUBLANES, LRU_WIDTH), u.reshape(groups, SUBLANES, LRU_WIDTH), False)
    hs, h = _chain(a, u, carry_f[0:1, :], False)
    hf_ref[...] = hs.reshape(t, LRU_WIDTH)
    carry_f[0:1, :] = h
    bufx[0:pad, :] = bufx[t:t + pad, :]

    bufy[0:t, :] = xb_ref[...].astype(F32)
    xc = lb_ref[1:2, :] + sum(lw_ref[1, k:k + 1, :] * bufy[k:k + t, :] for k in range(LRU_CONV_K))
    a, u = _lru_gates(xc, wai_ref, ba_ref, bi_ref, lam_ref, 1)
    a, u = _local_scan(a.reshape(groups, SUBLANES, LRU_WIDTH), u.reshape(groups, SUBLANES, LRU_WIDTH), True)
    hs, h = _chain(a, u, carry_b[0:1, :], True)
    hb_ref[...] = hs.reshape(t, LRU_WIDTH)
    carry_b[0:1, :] = h
    bufy[t:t + pad, :] = bufy[0:pad, :]


def _mixers(z3, conv_w, lru_conv_w, lru_conv_b, wai, b_a, b_i, lam):
    b, s, _ = z3.shape
    t = MIX_T
    nt = s // t
    halo = 16
    kern = functools.partial(_mixers_kernel, nt=nt)

    def zcol(c):
        return pl.BlockSpec((None, t, COL), lambda bi, i: (bi, i, c))

    def znext(c):
        return pl.BlockSpec((None, halo, COL),
                            lambda bi, i: (bi, jnp.minimum((i + 1) * (t // halo), s // halo - 1), c))

    def full(x):
        nd = x.ndim
        return pl.BlockSpec(x.shape, lambda bi, i: (0,) * nd)

    params = (conv_w, lru_conv_w, lru_conv_b, wai, b_a, b_i, lam)
    return pl.pallas_call(
        kern,
        grid=(b, nt),
        in_specs=[
            zcol(COL_CONV_B), zcol(COL_CONV_C), zcol(COL_CONV_X), zcol(COL_G_CONV), zcol(COL_LRU_X),
            pl.BlockSpec((None, t, COL), lambda bi, i: (bi, nt - 1 - i, COL_LRU_X)),
            znext(COL_CONV_C), znext(COL_CONV_X),
        ] + [full(x) for x in params],
        out_specs=[
            pl.BlockSpec((None, t, CONV_WIDTH), lambda bi, i: (bi, i, 0)),
            pl.BlockSpec((None, t, LRU_WIDTH), lambda bi, i: (bi, i, 0)),
            pl.BlockSpec((None, t, LRU_WIDTH), lambda bi, i: (bi, nt - 1 - i, 0)),
        ],
        out_shape=[
            jax.ShapeDtypeStruct((b, s, CONV_WIDTH), BF16),
            jax.ShapeDtypeStruct((b, s, LRU_WIDTH), F32),
            jax.ShapeDtypeStruct((b, s, LRU_WIDTH), F32),
        ],
        scratch_shapes=[
            pltpu.VMEM((t + 2 * SUBLANES, CONV_WIDTH), F32),
            pltpu.VMEM((t + SUBLANES, LRU_WIDTH), F32),
            pltpu.VMEM((t + SUBLANES, LRU_WIDTH), F32),
            pltpu.VMEM((SUBLANES, LRU_WIDTH), F32),
            pltpu.VMEM((SUBLANES, LRU_WIDTH), F32),
        ],
        compiler_params=pltpu.CompilerParams(
            dimension_semantics=("arbitrary", "arbitrary"), vmem_limit_bytes=VMEM_LIMIT),
        name="mixers",
    )(z3, z3, z3, z3, z3, z3, z3, z3, *params)


def _outple_kernel(h_ref, ya_ref, yc_ref, hf_ref, hb_ref, gl_ref, p_ref,
                   wo_ref, pn_ref, wg_ref, wp_ref, fn_ref, o_ref, *, last):
    y_lru = (hf_ref[...] + hb_ref[...]) * _silu(gl_ref[...].astype(F32))
    mix = jnp.concatenate([ya_ref[...], yc_ref[...], y_lru.astype(BF16)], axis=1)
    h1 = h_ref[...] + jnp.dot(mix, wo_ref[...], preferred_element_type=F32)
    n = _rms(h1, pn_ref[...]).astype(BF16)
    gate = _sigmoid(jnp.dot(n, wg_ref[...], preferred_element_type=F32))
    pp = jnp.dot(p_ref[...].astype(BF16), wp_ref[...], preferred_element_type=F32)
    h2 = h1 + gate * pp
    if last:
        h2 = _rms(h2, fn_ref[...])
    o_ref[...] = h2


def _outple(h2d, ya, yc, hf, hb, z2d, p2d, wo_b, pn, wg_b, wp_b, fn, last):
    m = h2d.shape[0]
    tm = OUT_TM
    kern = functools.partial(_outple_kernel, last=last)

    def rows(width, c=0):
        return pl.BlockSpec((tm, width), lambda i: (i, c))

    def resident(x):
        return pl.BlockSpec(x.shape, lambda i: (0, 0), pipeline_mode=pl.Buffered(1))

    return pl.pallas_call(
        kern,
        grid=(m // tm,),
        in_specs=[
            rows(D_MODEL), rows(ATTN_WIDTH), rows(CONV_WIDTH), rows(LRU_WIDTH), rows(LRU_WIDTH),
            rows(COL, COL_G_LRU), rows(PLE_DIM),
            resident(wo_b), resident(pn), resident(wg_b), resident(wp_b), resident(fn),
        ],
        out_specs=rows(D_MODEL),
        out_shape=jax.ShapeDtypeStruct((m, D_MODEL), F32),
        compiler_params=pltpu.CompilerParams(
            dimension_semantics=("arbitrary",), vmem_limit_bytes=VMEM_LIMIT),
        name="outple",
    )(h2d, ya, yc, hf, hb, z2d, p2d, wo_b, pn, wg_b, wp_b, fn)


def _trunk(x, p, prm, bias):
    b, s, _ = x.shape
    m = b * s
    depth = p.shape[0]
    h = x.reshape(m, D_MODEL)
    for l in range(depth):
        z = _inproj(h, prm["norm_mix"][l][None, :], prm["w_in"][l])
        z3 = z.reshape(b, s, IN_WIDTH)
        ya = _attention(z3, bias, prm["attn_sink"][l])
        yc, hf, hb = _mixers(z3, prm["conv_w"][l], prm["lru_conv_w"][l], prm["lru_conv_b"][l],
                             prm["wai"][l], prm["lru_b_a"][l], prm["lru_b_i"][l], prm["lru_L"][l])
        h = _outple(h, ya.reshape(m, ATTN_WIDTH), yc.reshape(m, CONV_WIDTH),
                    hf.reshape(m, LRU_WIDTH), hb.reshape(m, LRU_WIDTH), z,
                    p[l].reshape(m, PLE_DIM), prm["w_out"][l], prm["ple_norm"][l][None, :],
                    prm["ple_w_gate"][l], prm["ple_w_proj"][l], prm["final_norm"][None, :],
                    last=(l == depth - 1))
    return h.reshape(b, s, D_MODEL)


def kernel(x_prompt, x_sample, p_prompt, p_sample, norm_mix, w_in, w_out, rel_bias, attn_sink, conv_w,
           lru_conv_w, lru_conv_b, lru_w_a, lru_b_a, lru_w_i, lru_b_i, lru_L, ple_norm, ple_w_gate,
           ple_w_proj, final_norm):
    prm = dict(
        norm_mix=norm_mix, w_in=w_in.astype(BF16), w_out=w_out.astype(BF16), attn_sink=attn_sink,
        conv_w=conv_w, lru_conv_w=lru_conv_w, lru_conv_b=lru_conv_b,
        wai=jnp.concatenate([lru_w_a, lru_w_i], axis=-1).astype(BF16),
        lru_b_a=lru_b_a, lru_b_i=lru_b_i, lru_L=lru_L, ple_norm=ple_norm,
        ple_w_gate=ple_w_gate.astype(BF16), ple_w_proj=ple_w_proj.astype(BF16), final_norm=final_norm)
    bias = _rel_bias(rel_bias)
    y_prompt = _trunk(x_prompt, p_prompt, prm, bias)
    y_sample = _trunk(x_sample, p_sample, prm, bias)
    return (y_prompt, y_sample)
```

```python
import functools
import math

import numpy as np
import jax
import jax.numpy as jnp
from jax import lax
from jax.experimental import pallas as pl
from jax.experimental.pallas import tpu as pltpu

F32 = jnp.float32
BF16 = jnp.bfloat16

D_MODEL = 2048
N_HEADS = 8
N_KV_HEADS = 2
Q_GROUP = N_HEADS // N_KV_HEADS
HEAD_DIM = 128
ATTN_WIDTH = N_HEADS * HEAD_DIM
KV_WIDTH = N_KV_HEADS * HEAD_DIM
WINDOW = 128
BLOCK = 128
N_BUCKETS = 32
MAX_DISTANCE = 128
CONV_WIDTH = 512
CONV_K = 3
LRU_WIDTH = 512
LRU_HEADS = 4
LRU_BLOCK = 128
LRU_CONV_K = 4
LRU_C = 8.0
MIX_WIDTH = ATTN_WIDTH + CONV_WIDTH + LRU_WIDTH
IN_WIDTH = 2 * ATTN_WIDTH + 2 * KV_WIDTH + 4 * CONV_WIDTH + 2 * LRU_WIDTH
PLE_DIM = 256
NORM_EPS = 1e-6
NEG_INF = -1e30
LOG2E = math.log2(math.e)

COL = 512
COL_KV = 2
COL_G_ATTN = 3
COL_CONV_B = 5
COL_CONV_C = 6
COL_CONV_X = 7
COL_G_CONV = 8
COL_LRU_X = 9
COL_G_LRU = 10

SUBLANES = 8
VMEM_LIMIT = 56 * 1024 * 1024

INPROJ_TM = 512
INPROJ_TN = 512
ATTN_QB = 4
MIX_T = 256
OUT_TM = 256


def _sigmoid(x):
    return 1.0 / (1.0 + jnp.exp(-x))


def _silu(x):
    return x * _sigmoid(x)


def _rms(x, g):
    ms = jnp.mean(x * x, axis=-1, keepdims=True)
    return x * lax.rsqrt(ms + NORM_EPS) * g


def _inproj_kernel(h_ref, g_ref, w_ref, z_ref):
    u = _rms(h_ref[...], g_ref[...]).astype(BF16)
    for c in range(IN_WIDTH // INPROJ_TN):
        cols = slice(c * INPROJ_TN, (c + 1) * INPROJ_TN)
        z_ref[:, cols] = jnp.dot(u, w_ref[:, cols], preferred_element_type=F32).astype(z_ref.dtype)


def _layer_resident(x, l):
    nd = x.ndim - 1
    return pl.BlockSpec((None,) + x.shape[1:], lambda *_: (l,) + (0,) * nd, pipeline_mode=pl.Buffered(1))


def _inproj(h2d, g, w_b, l):
    m = h2d.shape[0]
    tm = INPROJ_TM
    return pl.pallas_call(
        _inproj_kernel,
        grid=(m // tm,),
        in_specs=[
            pl.BlockSpec((tm, D_MODEL), lambda i: (i, 0)),
            _layer_resident(g, l),
            _layer_resident(w_b, l),
        ],
        out_specs=pl.BlockSpec((tm, IN_WIDTH), lambda i: (i, 0)),
        out_shape=jax.ShapeDtypeStruct((m, IN_WIDTH), BF16),
        compiler_params=pltpu.CompilerParams(
            dimension_semantics=("arbitrary",), vmem_limit_bytes=VMEM_LIMIT),
        name="inproj",
    )(h2d, g, w_b)


def _bucket_table():
    q = np.arange(BLOCK)[:, None]
    c = np.arange(3 * BLOCK)[None, :]
    rel = c - BLOCK - q
    n = np.abs(rel)
    half = N_BUCKETS // 2
    max_exact = half // 2
    n2 = np.maximum(n, 1).astype(np.int64) ** 2
    floor_log2 = np.floor(np.log2(n2.astype(np.float64)) + 1e-9).astype(np.int64)
    large = np.minimum(max_exact + floor_log2 - 6, half - 1)
    bucket = np.where(rel > 0, half, 0) + np.where(n < max_exact, n, large)
    bucket = np.where(n <= WINDOW, bucket, -1)
    return bucket.astype(np.int32)


BIAS_INTERIOR, BIAS_FIRST, BIAS_LAST = 0, 1, 2


def _bias_kernel(tab_ref, bkt_ref, o_ref):
    v = pl.program_id(0)
    h = pl.program_id(1)
    b = bkt_ref[...]
    acc = jnp.full(b.shape, NEG_INF, F32)
    for k in range(N_BUCKETS):
        acc = jnp.where(b == k, tab_ref[k, h] * LOG2E, acc)
    col = lax.broadcasted_iota(jnp.int32, b.shape, 1)
    outside = ((col < BLOCK) & (v == BIAS_FIRST)) | ((col >= 2 * BLOCK) & (v == BIAS_LAST))
    o_ref[0, 0] = jnp.where(outside, NEG_INF, acc)


def _rel_bias(rel_table):
    bkt = jnp.asarray(_bucket_table())
    return pl.pallas_call(
        _bias_kernel,
        grid=(3, N_HEADS),
        in_specs=[
            pl.BlockSpec(memory_space=pltpu.SMEM),
            pl.BlockSpec((BLOCK, 3 * BLOCK), lambda v, h: (0, 0)),
        ],
        out_specs=pl.BlockSpec((1, 1, BLOCK, 3 * BLOCK), lambda v, h: (v, h, 0, 0)),
        out_shape=jax.ShapeDtypeStruct((3, N_HEADS, BLOCK, 3 * BLOCK), F32),
        name="relbias",
    )(rel_table, bkt)


def _attn_kernel(sink_ref, q_ref, kvp_ref, kvc_ref, kvn_ref, g0_ref, g1_ref, bias_ref, o_ref, *, nblk, layer):
    i = pl.program_id(1)
    scale2 = HEAD_DIM ** -0.5 * LOG2E
    g_refs = (g0_ref, g1_ref)
    ones = jnp.ones((3 * BLOCK, HEAD_DIM), BF16)
    for kh in range(N_KV_HEADS):
        ks = slice(kh * HEAD_DIM, (kh + 1) * HEAD_DIM)
        vs = slice(KV_WIDTH + kh * HEAD_DIM, KV_WIDTH + (kh + 1) * HEAD_DIM)
        k_win = jnp.concatenate([kvp_ref[:, ks], kvc_ref[:, ks], kvn_ref[:, ks]], axis=0)
        v_win = jnp.concatenate([kvp_ref[:, vs], kvc_ref[:, vs], kvn_ref[:, vs]], axis=0)
        for s in range(ATTN_QB):
            n = i * ATTN_QB + s
            variant = jnp.where(n == 0, BIAS_FIRST, jnp.where(n == nblk - 1, BIAS_LAST, BIAS_INTERIOR))
            bias4 = bias_ref[variant, kh * Q_GROUP:(kh + 1) * Q_GROUP].reshape(Q_GROUP * BLOCK, 3 * BLOCK)
            rows = slice(s * BLOCK, (s + 1) * BLOCK)
            q4 = jnp.concatenate(
                [q_ref[rows, (kh * Q_GROUP + g) * HEAD_DIM:(kh * Q_GROUP + g + 1) * HEAD_DIM]
                 for g in range(Q_GROUP)], axis=0)
            k_sub = k_win[s * BLOCK:s * BLOCK + 3 * BLOCK]
            v_ext = jnp.concatenate([v_win[s * BLOCK:s * BLOCK + 3 * BLOCK], ones], axis=1)
            sc = lax.dot_general(q4, k_sub, (((1,), (1,)), ((), ())), preferred_element_type=F32)
            sc = sc * scale2 + bias4
            es, sink_terms = [], []
            for g in range(Q_GROUP):
                sg = sc[g * BLOCK:(g + 1) * BLOCK]
                sink2 = sink_ref[layer, kh * Q_GROUP + g] * LOG2E
                m = jnp.maximum(jnp.max(sg, axis=-1, keepdims=True), sink2)
                es.append(jnp.exp2(sg - m).astype(BF16))
                sink_terms.append(jnp.exp2(sink2 - m))
            pv = jnp.dot(jnp.concatenate(es, axis=0), v_ext, preferred_element_type=F32)
            for g in range(Q_GROUP):
                hd = kh * Q_GROUP + g
                pg = pv[g * BLOCK:(g + 1) * BLOCK]
                o = pg[:, :HEAD_DIM] / (pg[:, HEAD_DIM:] + sink_terms[g])
                gcols = slice((hd % 4) * HEAD_DIM, (hd % 4 + 1) * HEAD_DIM)
                gate = g_refs[hd // 4][rows, gcols].astype(F32)
                o_ref[rows, hd * HEAD_DIM:(hd + 1) * HEAD_DIM] = (o * _silu(gate)).astype(o_ref.dtype)


def _attention(z3, bias, sink, layer):
    b, s, _ = z3.shape
    nblk = s // BLOCK
    assert nblk >= 2 and nblk % ATTN_QB == 0
    tq = ATTN_QB * BLOCK
    kern = functools.partial(_attn_kernel, nblk=nblk, layer=layer)
    return pl.pallas_call(
        kern,
        grid=(b, s // tq),
        in_specs=[
            pl.BlockSpec(memory_space=pltpu.SMEM),
            pl.BlockSpec((None, tq, ATTN_WIDTH), lambda bi, i: (bi, i, 0)),
            pl.BlockSpec((None, BLOCK, COL), lambda bi, i: (bi, jnp.maximum(i * ATTN_QB - 1, 0), COL_KV)),
            pl.BlockSpec((None, tq, COL), lambda bi, i: (bi, i, COL_KV)),
            pl.BlockSpec((None, BLOCK, COL),
                         lambda bi, i: (bi, jnp.minimum((i + 1) * ATTN_QB, nblk - 1), COL_KV)),
            pl.BlockSpec((None, tq, COL), lambda bi, i: (bi, i, COL_G_ATTN)),
            pl.BlockSpec((None, tq, COL), lambda bi, i: (bi, i, COL_G_ATTN + 1)),
            pl.BlockSpec(bias.shape, lambda bi, i: (0, 0, 0, 0)),
        ],
        out_specs=pl.BlockSpec((None, tq, ATTN_WIDTH), lambda bi, i: (bi, i, 0)),
        out_shape=jax.ShapeDtypeStruct((b, s, ATTN_WIDTH), BF16),
        compiler_params=pltpu.CompilerParams(
            dimension_semantics=("arbitrary", "arbitrary"), vmem_limit_bytes=VMEM_LIMIT),
        name="attn",
    )(sink, z3, z3, z3, z3, z3, z3, bias)


def _local_scan(a, u, reverse):
    row = lax.broadcasted_iota(jnp.int32, (1, SUBLANES, a.shape[-1]), 1)
    d = 1
    while d < SUBLANES:
        shift = SUBLANES - d if reverse else d
        a_sh = pltpu.roll(a, shift, axis=1)
        u_sh = pltpu.roll(u, shift, axis=1)
        keep = (row < SUBLANES - d) if reverse else (row >= d)
        u = jnp.where(keep, a * u_sh + u, u)
        a = jnp.where(keep, a * a_sh, a)
        d *= 2
    return a, u


def _chain(a, u, h, reverse):
    groups = a.shape[0]
    out = [None] * groups
    order = range(groups - 1, -1, -1) if reverse else range(groups)
    edge = 0 if reverse else SUBLANES - 1
    for g in order:
        hg = a[g] * h + u[g]
        out[g] = hg
        h = hg[edge:edge + 1, :]
    return jnp.stack(out, axis=0), h


def _lru_gates(xc, wai_ref, ba_ref, bi_ref, lam_ref, d):
    ra, ri = [], []
    for hd in range(LRU_HEADS):
        xh = xc[:, hd * LRU_BLOCK:(hd + 1) * LRU_BLOCK].astype(BF16)
        y = jnp.dot(xh, wai_ref[d, hd], preferred_element_type=F32)
        ra.append(y[:, :LRU_BLOCK])
        ri.append(y[:, LRU_BLOCK:])
    r = _sigmoid(jnp.concatenate(ra, axis=1) + ba_ref[d:d + 1, :])
    gi = _sigmoid(jnp.concatenate(ri, axis=1) + bi_ref[d:d + 1, :])
    x = -lam_ref[d:d + 1, :]
    softplus = jnp.maximum(x, 0.0) + jnp.log1p(jnp.exp(-jnp.abs(x)))
    log_a = (-LRU_C * softplus) * r
    a = jnp.exp(log_a)
    u = jnp.sqrt(-jnp.tanh(log_a) * (a * a + 1.0)) * (gi * xc)
    return a, u


def _mixers_kernel(cb_ref, cc_ref, cx_ref, gc_ref, xf_ref, xb_ref, ccn_ref, cxn_ref,
                   cw_ref, lw_ref, lb_ref, wai_ref, ba_ref, bi_ref, lam_ref,
                   yc_ref, hf_ref, hb_ref,
                   bufc, bufx, bufy, carry_f, carry_b, *, nt):
    i = pl.program_id(1)
    t = MIX_T
    groups = t // SUBLANES
    pad = SUBLANES

    @pl.when(i == 0)
    def _():
        zeros = jnp.zeros((pad, LRU_WIDTH), F32)
        bufc[0:pad, :] = zeros
        bufx[0:pad, :] = zeros
        bufy[t:t + pad, :] = zeros
        carry_f[...] = zeros
        carry_b[...] = zeros

    bufc[pad:pad + t, :] = cc_ref[...].astype(F32) * cx_ref[...].astype(F32)
    nxt = ccn_ref[0:pad, :].astype(F32) * cxn_ref[0:pad, :].astype(F32)
    bufc[pad + t:pad + t + pad, :] = jnp.where(i == nt - 1, 0.0, nxt)
    conv = (cw_ref[0:1, :] * bufc[pad - 1:pad - 1 + t, :]
            + cw_ref[1:2, :] * bufc[pad:pad + t, :]
            + cw_ref[2:3, :] * bufc[pad + 1:pad + 1 + t, :])
    yc_ref[...] = (cb_ref[...].astype(F32) * conv * _silu(gc_ref[...].astype(F32))).astype(yc_ref.dtype)
    bufc[0:pad, :] = bufc[t:t + pad, :]

    bufx[pad:pad + t, :] = xf_ref[...].astype(F32)
    xc = lb_ref[0:1, :] + sum(
        lw_ref[0, k:k + 1, :] * bufx[pad - (LRU_CONV_K - 1) + k:pad - (LRU_CONV_K - 1) + k + t, :]
        for k in range(LRU_CONV_K))
    a, u = _lru_gates(xc, wai_ref, ba_ref, bi_ref, lam_ref, 0)
    a, u = _local_scan(a.reshape(groups, SUBLANES, LRU_WIDTH), u.reshape(groups, SUBLANES, LRU_WIDTH), False)
    hs, h = _chain(a, u, carry_f[0:1, :], False)
    hf_ref[...] = hs.reshape(t, LRU_WIDTH)
    carry_f[0:1, :] = h
    bufx[0:pad, :] = bufx[t:t + pad, :]

    bufy[0:t, :] = xb_ref[...].astype(F32)
    xc = lb_ref[1:2, :] + sum(lw_ref[1, k:k + 1, :] * bufy[k:k + t, :] for k in range(LRU_CONV_K))
    a, u = _lru_gates(xc, wai_ref, ba_ref, bi_ref, lam_ref, 1)
    a, u = _local_scan(a.reshape(groups, SUBLANES, LRU_WIDTH), u.reshape(groups, SUBLANES, LRU_WIDTH), True)
    hs, h = _chain(a, u, carry_b[0:1, :], True)
    hb_ref[...] = hs.reshape(t, LRU_WIDTH)
    carry_b[0:1, :] = h
    bufy[t:t + pad, :] = bufy[0:pad, :]


def _mixers(z3, conv_w, lru_conv_w, lru_conv_b, wai, b_a, b_i, lam, l):
    b, s, _ = z3.shape
    t = MIX_T
    nt = s // t
    halo = 16
    kern = functools.partial(_mixers_kernel, nt=nt)

    def zcol(c):
        return pl.BlockSpec((None, t, COL), lambda bi, i: (bi, i, c))

    def znext(c):
        return pl.BlockSpec((None, halo, COL),
                            lambda bi, i: (bi, jnp.minimum((i + 1) * (t // halo), s // halo - 1), c))

    params = (conv_w, lru_conv_w, lru_conv_b, wai, b_a, b_i, lam)
    return pl.pallas_call(
        kern,
        grid=(b, nt),
        in_specs=[
            zcol(COL_CONV_B), zcol(COL_CONV_C), zcol(COL_CONV_X), zcol(COL_G_CONV), zcol(COL_LRU_X),
            pl.BlockSpec((None, t, COL), lambda bi, i: (bi, nt - 1 - i, COL_LRU_X)),
            znext(COL_CONV_C), znext(COL_CONV_X),
        ] + [_layer_resident(x, l) for x in params],
        out_specs=[
            pl.BlockSpec((None, t, CONV_WIDTH), lambda bi, i: (bi, i, 0)),
            pl.BlockSpec((None, t, LRU_WIDTH), lambda bi, i: (bi, i, 0)),
            pl.BlockSpec((None, t, LRU_WIDTH), lambda bi, i: (bi, nt - 1 - i, 0)),
        ],
        out_shape=[
            jax.ShapeDtypeStruct((b, s, CONV_WIDTH), BF16),
            jax.ShapeDtypeStruct((b, s, LRU_WIDTH), F32),
            jax.ShapeDtypeStruct((b, s, LRU_WIDTH), F32),
        ],
        scratch_shapes=[
            pltpu.VMEM((t + 2 * SUBLANES, CONV_WIDTH), F32),
            pltpu.VMEM((t + SUBLANES, LRU_WIDTH), F32),
            pltpu.VMEM((t + SUBLANES, LRU_WIDTH), F32),
            pltpu.VMEM((SUBLANES, LRU_WIDTH), F32),
            pltpu.VMEM((SUBLANES, LRU_WIDTH), F32),
        ],
        compiler_params=pltpu.CompilerParams(
            dimension_semantics=("arbitrary", "arbitrary"), vmem_limit_bytes=VMEM_LIMIT),
        name="mixers",
    )(z3, z3, z3, z3, z3, z3, z3, z3, *params)


def _outple_kernel(h_ref, ya_ref, yc_ref, hf_ref, hb_ref, gl_ref, p_ref,
                   wo_ref, pn_ref, wg_ref, wp_ref, fn_ref, o_ref, *, last):
    y_lru = (hf_ref[...] + hb_ref[...]) * _silu(gl_ref[...].astype(F32))
    mix = jnp.concatenate([ya_ref[...], yc_ref[...], y_lru.astype(BF16)], axis=1)
    h1 = h_ref[...] + jnp.dot(mix, wo_ref[...], preferred_element_type=F32)
    n = _rms(h1, pn_ref[...]).astype(BF16)
    gate = _sigmoid(jnp.dot(n, wg_ref[...], preferred_element_type=F32))
    pp = jnp.dot(p_ref[...].astype(BF16), wp_ref[...], preferred_element_type=F32)
    h2 = h1 + gate * pp
    if last:
        h2 = _rms(h2, fn_ref[...])
    o_ref[...] = h2


def _outple(h2d, ya, yc, hf, hb, z2d, p3d, wo_b, pn, wg_b, wp_b, fn, l, last):
    m = h2d.shape[0]
    tm = OUT_TM
    kern = functools.partial(_outple_kernel, last=last)

    def rows(width, c=0):
        return pl.BlockSpec((tm, width), lambda i: (i, c))

    return pl.pallas_call(
        kern,
        grid=(m // tm,),
        in_specs=[
            rows(D_MODEL), rows(ATTN_WIDTH), rows(CONV_WIDTH), rows(LRU_WIDTH), rows(LRU_WIDTH),
            rows(COL, COL_G_LRU), pl.BlockSpec((None, tm, PLE_DIM), lambda i: (l, i, 0)),
            _layer_resident(wo_b, l), _layer_resident(pn, l), _layer_resident(wg_b, l),
            _layer_resident(wp_b, l), _layer_resident(fn, 0),
        ],
        out_specs=rows(D_MODEL),
        out_shape=jax.ShapeDtypeStruct((m, D_MODEL), F32),
        compiler_params=pltpu.CompilerParams(
            dimension_semantics=("arbitrary",), vmem_limit_bytes=VMEM_LIMIT),
        name="outple",
    )(h2d, ya, yc, hf, hb, z2d, p3d, wo_b, pn, wg_b, wp_b, fn)


def _trunk(x, p, prm, bias):
    b, s, _ = x.shape
    m = b * s
    depth = p.shape[0]
    h = x.reshape(m, D_MODEL)
    p3d = p.reshape(depth, m, PLE_DIM)
    for l in range(depth):
        z = _inproj(h, prm["norm_mix"], prm["w_in"], l)
        z3 = z.reshape(b, s, IN_WIDTH)
        ya = _attention(z3, bias, prm["attn_sink"], l)
        yc, hf, hb = _mixers(z3, prm["conv_w"], prm["lru_conv_w"], prm["lru_conv_b"],
                             prm["wai"], prm["lru_b_a"], prm["lru_b_i"], prm["lru_L"], l)
        h = _outple(h, ya.reshape(m, ATTN_WIDTH), yc.reshape(m, CONV_WIDTH),
                    hf.reshape(m, LRU_WIDTH), hb.reshape(m, LRU_WIDTH), z, p3d,
                    prm["w_out"], prm["ple_norm"], prm["ple_w_gate"], prm["ple_w_proj"],
                    prm["final_norm"], l, last=(l == depth - 1))
    return h.reshape(b, s, D_MODEL)


def kernel(x_prompt, x_sample, p_prompt, p_sample, norm_mix, w_in, w_out, rel_bias, attn_sink, conv_w,
           lru_conv_w, lru_conv_b, lru_w_a, lru_b_a, lru_w_i, lru_b_i, lru_L, ple_norm, ple_w_gate,
           ple_w_proj, final_norm):
    prm = dict(
        norm_mix=norm_mix[:, None, :], w_in=w_in.astype(BF16), w_out=w_out.astype(BF16), attn_sink=attn_sink,
        conv_w=conv_w, lru_conv_w=lru_conv_w, lru_conv_b=lru_conv_b,
        wai=jnp.concatenate([lru_w_a, lru_w_i], axis=-1).astype(BF16),
        lru_b_a=lru_b_a, lru_b_i=lru_b_i, lru_L=lru_L, ple_norm=ple_norm[:, None, :],
        ple_w_gate=ple_w_gate.astype(BF16), ple_w_proj=ple_w_proj.astype(BF16),
        final_norm=final_norm[None, None, :])
    bias = _rel_bias(rel_bias)
    y_prompt = _trunk(x_prompt, p_prompt, prm, bias)
    y_sample = _trunk(x_sample, p_sample, prm, bias)
    return (y_prompt, y_sample)
```

```python
import functools
import math

import numpy as np
import jax
import jax.numpy as jnp
from jax import lax
from jax.experimental import pallas as pl
from jax.experimental.pallas import tpu as pltpu

F32 = jnp.float32
BF16 = jnp.bfloat16

D_MODEL = 2048
N_HEADS = 8
N_KV_HEADS = 2
Q_GROUP = N_HEADS // N_KV_HEADS
HEAD_DIM = 128
ATTN_WIDTH = N_HEADS * HEAD_DIM
KV_WIDTH = N_KV_HEADS * HEAD_DIM
WINDOW = 128
BLOCK = 128
N_BUCKETS = 32
MAX_DISTANCE = 128
CONV_WIDTH = 512
CONV_K = 3
LRU_WIDTH = 512
LRU_HEADS = 4
LRU_BLOCK = 128
LRU_CONV_K = 4
LRU_C = 8.0
MIX_WIDTH = ATTN_WIDTH + CONV_WIDTH + LRU_WIDTH
IN_WIDTH = 2 * ATTN_WIDTH + 2 * KV_WIDTH + 4 * CONV_WIDTH + 2 * LRU_WIDTH
PLE_DIM = 256
NORM_EPS = 1e-6
NEG_INF = -1e30
LOG2E = math.log2(math.e)

COL = 512
COL_KV = 2
COL_G_ATTN = 3
COL_CONV_B = 5
COL_CONV_C = 6
COL_CONV_X = 7
COL_G_CONV = 8
COL_LRU_X = 9
COL_G_LRU = 10

SUBLANES = 8
LANES = 128
VMEM_LIMIT = 60 * 1024 * 1024

INPROJ_TM = 512
INPROJ_TN = 512
ATTN_QB = 4
MIX_T = 512
MIX_CHUNK = 8
OUT_TM = 512
OUT_SUB = 256


def _sigmoid(x):
    return 1.0 / (1.0 + jnp.exp(-x))


def _silu(x):
    return x * _sigmoid(x)


def _rms(x, g):
    ms = jnp.mean(x * x, axis=-1, keepdims=True)
    return x * lax.rsqrt(ms + NORM_EPS) * g


def _inproj_kernel(h_ref, g_ref, w_ref, z_ref):
    u = _rms(h_ref[...], g_ref[...]).astype(BF16)
    for c in range(IN_WIDTH // INPROJ_TN):
        cols = slice(c * INPROJ_TN, (c + 1) * INPROJ_TN)
        z_ref[:, cols] = jnp.dot(u, w_ref[:, cols], preferred_element_type=F32).astype(z_ref.dtype)


def _layer_resident(x, l):
    nd = x.ndim - 1
    return pl.BlockSpec((None,) + x.shape[1:], lambda *_: (l,) + (0,) * nd, pipeline_mode=pl.Buffered(1))


def _inproj(h2d, g, w_b, l):
    m = h2d.shape[0]
    tm = INPROJ_TM
    return pl.pallas_call(
        _inproj_kernel,
        grid=(m // tm,),
        in_specs=[
            pl.BlockSpec((tm, D_MODEL), lambda i: (i, 0)),
            _layer_resident(g, l),
            _layer_resident(w_b, l),
        ],
        out_specs=pl.BlockSpec((tm, IN_WIDTH), lambda i: (i, 0)),
        out_shape=jax.ShapeDtypeStruct((m, IN_WIDTH), BF16),
        compiler_params=pltpu.CompilerParams(
            dimension_semantics=("arbitrary",), vmem_limit_bytes=VMEM_LIMIT),
        name="inproj",
    )(h2d, g, w_b)


def _bucket_table():
    q = np.arange(BLOCK)[:, None]
    c = np.arange(3 * BLOCK)[None, :]
    rel = c - BLOCK - q
    n = np.abs(rel)
    half = N_BUCKETS // 2
    max_exact = half // 2
    n2 = np.maximum(n, 1).astype(np.int64) ** 2
    floor_log2 = np.floor(np.log2(n2.astype(np.float64)) + 1e-9).astype(np.int64)
    large = np.minimum(max_exact + floor_log2 - 6, half - 1)
    bucket = np.where(rel > 0, half, 0) + np.where(n < max_exact, n, large)
    bucket = np.where(n <= WINDOW, bucket, -1)
    return bucket.astype(np.int32)


BIAS_INTERIOR, BIAS_FIRST, BIAS_LAST = 0, 1, 2


def _bias_kernel(tab_ref, bkt_ref, o_ref):
    v = pl.program_id(0)
    h = pl.program_id(1)
    b = bkt_ref[...]
    acc = jnp.full(b.shape, NEG_INF, F32)
    for k in range(N_BUCKETS):
        acc = jnp.where(b == k, tab_ref[k, h] * LOG2E, acc)
    col = lax.broadcasted_iota(jnp.int32, b.shape, 1)
    outside = ((col < BLOCK) & (v == BIAS_FIRST)) | ((col >= 2 * BLOCK) & (v == BIAS_LAST))
    o_ref[0, 0] = jnp.where(outside, NEG_INF, acc)


def _rel_bias(rel_table):
    bkt = jnp.asarray(_bucket_table())
    return pl.pallas_call(
        _bias_kernel,
        grid=(3, N_HEADS),
        in_specs=[
            pl.BlockSpec(memory_space=pltpu.SMEM),
            pl.BlockSpec((BLOCK, 3 * BLOCK), lambda v, h: (0, 0)),
        ],
        out_specs=pl.BlockSpec((1, 1, BLOCK, 3 * BLOCK), lambda v, h: (v, h, 0, 0)),
        out_shape=jax.ShapeDtypeStruct((3, N_HEADS, BLOCK, 3 * BLOCK), F32),
        name="relbias",
    )(rel_table, bkt)


def _attn_kernel(sink_ref, q_ref, kvp_ref, kvc_ref, kvn_ref, g0_ref, g1_ref, bias_ref, o_ref, *, nblk, layer):
    i = pl.program_id(1)
    scale2 = HEAD_DIM ** -0.5 * LOG2E
    g_refs = (g0_ref, g1_ref)
    ones = jnp.ones((3 * BLOCK, HEAD_DIM), BF16)
    for kh in range(N_KV_HEADS):
        ks = slice(kh * HEAD_DIM, (kh + 1) * HEAD_DIM)
        vs = slice(KV_WIDTH + kh * HEAD_DIM, KV_WIDTH + (kh + 1) * HEAD_DIM)
        k_win = jnp.concatenate([kvp_ref[:, ks], kvc_ref[:, ks], kvn_ref[:, ks]], axis=0)
        v_win = jnp.concatenate([kvp_ref[:, vs], kvc_ref[:, vs], kvn_ref[:, vs]], axis=0)
        for s in range(ATTN_QB):
            n = i * ATTN_QB + s
            variant = jnp.where(n == 0, BIAS_FIRST, jnp.where(n == nblk - 1, BIAS_LAST, BIAS_INTERIOR))
            bias4 = bias_ref[variant, kh * Q_GROUP:(kh + 1) * Q_GROUP].reshape(Q_GROUP * BLOCK, 3 * BLOCK)
            rows = slice(s * BLOCK, (s + 1) * BLOCK)
            q4 = jnp.concatenate(
                [q_ref[rows, (kh * Q_GROUP + g) * HEAD_DIM:(kh * Q_GROUP + g + 1) * HEAD_DIM]
                 for g in range(Q_GROUP)], axis=0)
            k_sub = k_win[s * BLOCK:s * BLOCK + 3 * BLOCK]
            v_ext = jnp.concatenate([v_win[s * BLOCK:s * BLOCK + 3 * BLOCK], ones], axis=1)
            sc = lax.dot_general(q4, k_sub, (((1,), (1,)), ((), ())), preferred_element_type=F32)
            sc = sc * scale2 + bias4
            es, sink_terms = [], []
            for g in range(Q_GROUP):
                sg = sc[g * BLOCK:(g + 1) * BLOCK]
                sink2 = sink_ref[layer, kh * Q_GROUP + g] * LOG2E
                m = jnp.maximum(jnp.max(sg, axis=-1, keepdims=True), sink2)
                es.append(jnp.exp2(sg - m).astype(BF16))
                sink_terms.append(jnp.exp2(sink2 - m))
            pv = jnp.dot(jnp.concatenate(es, axis=0), v_ext, preferred_element_type=F32)
            for g in range(Q_GROUP):
                hd = kh * Q_GROUP + g
                pg = pv[g * BLOCK:(g + 1) * BLOCK]
                o = pg[:, :HEAD_DIM] / (pg[:, HEAD_DIM:] + sink_terms[g])
                gcols = slice((hd % 4) * HEAD_DIM, (hd % 4 + 1) * HEAD_DIM)
                gate = g_refs[hd // 4][rows, gcols].astype(F32)
                o_ref[rows, hd * HEAD_DIM:(hd + 1) * HEAD_DIM] = (o * _silu(gate)).astype(o_ref.dtype)


def _attention(z3, bias, sink, layer):
    b, s, _ = z3.shape
    nblk = s // BLOCK
    assert nblk >= 2 and nblk % ATTN_QB == 0
    tq = ATTN_QB * BLOCK
    kern = functools.partial(_attn_kernel, nblk=nblk, layer=layer)
    return pl.pallas_call(
        kern,
        grid=(b, s // tq),
        in_specs=[
            pl.BlockSpec(memory_space=pltpu.SMEM),
            pl.BlockSpec((None, tq, ATTN_WIDTH), lambda bi, i: (bi, i, 0)),
            pl.BlockSpec((None, BLOCK, COL), lambda bi, i: (bi, jnp.maximum(i * ATTN_QB - 1, 0), COL_KV)),
            pl.BlockSpec((None, tq, COL), lambda bi, i: (bi, i, COL_KV)),
            pl.BlockSpec((None, BLOCK, COL),
                         lambda bi, i: (bi, jnp.minimum((i + 1) * ATTN_QB, nblk - 1), COL_KV)),
            pl.BlockSpec((None, tq, COL), lambda bi, i: (bi, i, COL_G_ATTN)),
            pl.BlockSpec((None, tq, COL), lambda bi, i: (bi, i, COL_G_ATTN + 1)),
            pl.BlockSpec(bias.shape, lambda bi, i: (0, 0, 0, 0)),
        ],
        out_specs=pl.BlockSpec((None, tq, ATTN_WIDTH), lambda bi, i: (bi, i, 0)),
        out_shape=jax.ShapeDtypeStruct((b, s, ATTN_WIDTH), BF16),
        compiler_params=pltpu.CompilerParams(
            dimension_semantics=("arbitrary", "arbitrary"), vmem_limit_bytes=VMEM_LIMIT),
        name="attn",
    )(sink, z3, z3, z3, z3, z3, z3, bias)


def _segment_permutation(t):
    groups = t // SUBLANES
    i = np.arange(t)
    p = np.zeros((t, t), np.float32)
    p[i, (i % SUBLANES) * groups + i // SUBLANES] = 1.0
    return p


def _lru_gates(xc, log_a_scale, wai_ref, ba_ref, bi_ref, d):
    ra, ri = [], []
    for hd in range(LRU_HEADS):
        xh = xc[:, hd * LRU_BLOCK:(hd + 1) * LRU_BLOCK].astype(BF16)
        y = jnp.dot(xh, wai_ref[d, hd], preferred_element_type=F32)
        ra.append(y[:, :LRU_BLOCK])
        ri.append(y[:, LRU_BLOCK:])
    r = _sigmoid(jnp.concatenate(ra, axis=1) + ba_ref[d:d + 1, :])
    gi = _sigmoid(jnp.concatenate(ri, axis=1) + bi_ref[d:d + 1, :])
    log_a = log_a_scale * r
    a = jnp.exp(log_a)
    v = -jnp.tanh(log_a) * (a * a + 1.0)
    root = jnp.where(v > 0.0, v * lax.rsqrt(v), 0.0)
    return a, root * (gi * xc)


def _lru_direction(x_ref, perm_ref, halo, carry, a_scr, u_scr, out_ref,
                   lw_ref, lb_ref, wai_ref, ba_ref, bi_ref, lam_ref, d):
    reverse = d == 1
    t = MIX_T
    groups = t // SUBLANES
    w = LRU_WIDTH
    k = LRU_CONV_K - 1
    x = -lam_ref[d:d + 1, :]
    log_a_scale = -LRU_C * (jnp.maximum(x, 0.0) + jnp.log1p(jnp.exp(-jnp.abs(x))))
    x3 = jnp.dot(perm_ref[...], x_ref[...], preferred_element_type=F32).reshape(groups, SUBLANES, w)
    row = lax.broadcasted_iota(jnp.int32, (1, SUBLANES, w), 1)
    if reverse:
        src = x3[:k]
        wrapped = pltpu.roll(jnp.where(row == 0, halo[...], src), SUBLANES - 1, axis=1)
        ext = jnp.concatenate([x3, wrapped], axis=0)
    else:
        src = x3[groups - k:]
        wrapped = pltpu.roll(jnp.where(row == SUBLANES - 1, halo[...], src), 1, axis=1)
        ext = jnp.concatenate([wrapped, x3], axis=0)
    halo[...] = src
    taps = [lw_ref[d, j:j + 1, :] for j in range(LRU_CONV_K)]
    bias = lb_ref[d:d + 1, :]

    def in_time_order(n):
        return range(n - 1, -1, -1) if reverse else range(n)

    h = p = None
    for ci in in_time_order(groups // MIX_CHUNK):
        g0 = ci * MIX_CHUNK
        xc = bias + sum(taps[j] * ext[g0 + j:g0 + j + MIX_CHUNK] for j in range(LRU_CONV_K))
        a, u = _lru_gates(xc.reshape(MIX_CHUNK * SUBLANES, w), log_a_scale, wai_ref, ba_ref, bi_ref, d)
        a3 = a.reshape(MIX_CHUNK, SUBLANES, w)
        u3 = u.reshape(MIX_CHUNK, SUBLANES, w)
        a_scr[g0:g0 + MIX_CHUNK] = a3
        u_scr[g0:g0 + MIX_CHUNK] = u3
        for g in in_time_order(MIX_CHUNK):
            h = u3[g] if h is None else a3[g] * h + u3[g]
            p = a3[g] if p is None else a3[g] * p

    c = carry[0:1, :]
    cs = [None] * SUBLANES
    for r in in_time_order(SUBLANES):
        cs[r] = c
        c = p[r:r + 1, :] * c + h[r:r + 1, :]
    carry[0:1, :] = c

    h = jnp.concatenate(cs, axis=0)
    for g in in_time_order(groups):
        h = a_scr[g] * h + u_scr[g]
        for j in range(w // LANES):
            out_ref[j, pl.ds(g, SUBLANES, stride=groups), :] = h[:, j * LANES:(j + 1) * LANES]


def _mixers_kernel(cb_ref, cc_ref, cx_ref, gc_ref, xf_ref, xb_ref, ccn_ref, cxn_ref, perm_ref,
                   cw_ref, lw_ref, lb_ref, wai_ref, ba_ref, bi_ref, lam_ref,
                   yc_ref, hf_ref, hb_ref,
                   bufc, halo_f, halo_b, carry_f, carry_b, a_f, u_f, a_b, u_b, *, nt):
    i = pl.program_id(1)
    t = MIX_T
    pad = SUBLANES

    @pl.when(i == 0)
    def _():
        bufc[0:pad, :] = jnp.zeros((pad, CONV_WIDTH), F32)
        halo_f[...] = jnp.zeros(halo_f.shape, F32)
        halo_b[...] = jnp.zeros(halo_b.shape, F32)
        carry_f[...] = jnp.zeros(carry_f.shape, F32)
        carry_b[...] = jnp.zeros(carry_b.shape, F32)

    rows_per_chunk = MIX_CHUNK * SUBLANES
    for r0 in range(0, t, rows_per_chunk):
        rows = slice(r0, r0 + rows_per_chunk)
        bufc[pad + r0:pad + r0 + rows_per_chunk, :] = cc_ref[rows, :].astype(F32) * cx_ref[rows, :].astype(F32)
    nxt = ccn_ref[0:pad, :].astype(F32) * cxn_ref[0:pad, :].astype(F32)
    bufc[pad + t:pad + t + pad, :] = jnp.where(i == nt - 1, 0.0, nxt)
    for r0 in range(0, t, rows_per_chunk):
        conv = sum(cw_ref[j:j + 1, :] * bufc[pad - 1 + j + r0:pad - 1 + j + r0 + rows_per_chunk, :]
                   for j in range(CONV_K))
        rows = slice(r0, r0 + rows_per_chunk)
        gated = cb_ref[rows, :].astype(F32) * conv * _silu(gc_ref[rows, :].astype(F32))
        yc_ref[rows, :] = gated.astype(yc_ref.dtype)
    bufc[0:pad, :] = bufc[t:t + pad, :]

    lru = (lw_ref, lb_ref, wai_ref, ba_ref, bi_ref, lam_ref)
    _lru_direction(xf_ref, perm_ref, halo_f, carry_f, a_f, u_f, hf_ref, *lru, 0)
    _lru_direction(xb_ref, perm_ref, halo_b, carry_b, a_b, u_b, hb_ref, *lru, 1)


def _mixers(z3, conv_w, lru_conv_w, lru_conv_b, wai, b_a, b_i, lam, l):
    b, s, _ = z3.shape
    t = MIX_T
    nt = s // t
    halo = 16
    slabs = LRU_WIDTH // LANES
    kern = functools.partial(_mixers_kernel, nt=nt)
    perm = jnp.asarray(_segment_permutation(t), BF16)

    def zcol(c):
        return pl.BlockSpec((None, t, COL), lambda bi, i: (bi, i, c))

    def znext(c):
        return pl.BlockSpec((None, halo, COL),
                            lambda bi, i: (bi, jnp.minimum((i + 1) * (t // halo), s // halo - 1), c))

    params = (conv_w, lru_conv_w, lru_conv_b, wai, b_a, b_i, lam)
    return pl.pallas_call(
        kern,
        grid=(b, nt),
        in_specs=[
            zcol(COL_CONV_B), zcol(COL_CONV_C), zcol(COL_CONV_X), zcol(COL_G_CONV), zcol(COL_LRU_X),
            pl.BlockSpec((None, t, COL), lambda bi, i: (bi, nt - 1 - i, COL_LRU_X)),
            znext(COL_CONV_C), znext(COL_CONV_X),
            pl.BlockSpec((t, t), lambda bi, i: (0, 0)),
        ] + [_layer_resident(x, l) for x in params],
        out_specs=[
            pl.BlockSpec((None, t, CONV_WIDTH), lambda bi, i: (bi, i, 0)),
            pl.BlockSpec((slabs, None, t, LANES), lambda bi, i: (0, bi, i, 0)),
            pl.BlockSpec((slabs, None, t, LANES), lambda bi, i: (0, bi, nt - 1 - i, 0)),
        ],
        out_shape=[
            jax.ShapeDtypeStruct((b, s, CONV_WIDTH), BF16),
            jax.ShapeDtypeStruct((slabs, b, s, LANES), F32),
            jax.ShapeDtypeStruct((slabs, b, s, LANES), F32),
        ],
        scratch_shapes=[
            pltpu.VMEM((t + 2 * SUBLANES, CONV_WIDTH), F32),
            pltpu.VMEM((LRU_CONV_K - 1, SUBLANES, LRU_WIDTH), F32),
            pltpu.VMEM((LRU_CONV_K - 1, SUBLANES, LRU_WIDTH), F32),
            pltpu.VMEM((SUBLANES, LRU_WIDTH), F32),
            pltpu.VMEM((SUBLANES, LRU_WIDTH), F32),
        ] + [pltpu.VMEM((t // SUBLANES, SUBLANES, LRU_WIDTH), F32)] * 4,
        compiler_params=pltpu.CompilerParams(
            dimension_semantics=("arbitrary", "arbitrary"), vmem_limit_bytes=VMEM_LIMIT),
        name="mixers",
    )(z3, z3, z3, z3, z3, z3, z3, z3, perm, *params)


def _outple_kernel(h_ref, ya_ref, yc_ref, hf_ref, hb_ref, gl_ref, p_ref,
                   wo_ref, pn_ref, wg_ref, wp_ref, fn_ref, o_ref, *, last):
    for s in range(OUT_TM // OUT_SUB):
        rows = slice(s * OUT_SUB, (s + 1) * OUT_SUB)
        hsum = hf_ref[:, rows, :] + hb_ref[:, rows, :]
        y_lru = jnp.concatenate([hsum[j] for j in range(LRU_WIDTH // LANES)], axis=1)
        y_lru = y_lru * _silu(gl_ref[rows, :].astype(F32))
        mix = jnp.concatenate([ya_ref[rows, :], yc_ref[rows, :], y_lru.astype(BF16)], axis=1)
        h1 = h_ref[rows, :] + jnp.dot(mix, wo_ref[...], preferred_element_type=F32)
        n = _rms(h1, pn_ref[...]).astype(BF16)
        gate = _sigmoid(jnp.dot(n, wg_ref[...], preferred_element_type=F32))
        pp = jnp.dot(p_ref[rows, :].astype(BF16), wp_ref[...], preferred_element_type=F32)
        h2 = h1 + gate * pp
        if last:
            h2 = _rms(h2, fn_ref[...])
        o_ref[rows, :] = h2


def _outple(h2d, ya, yc, hf, hb, z2d, p3d, wo_b, pn, wg_b, wp_b, fn, l, last):
    m = h2d.shape[0]
    tm = OUT_TM
    kern = functools.partial(_outple_kernel, last=last)

    def rows(width, c=0):
        return pl.BlockSpec((tm, width), lambda i: (i, c))

    slab = pl.BlockSpec((LRU_WIDTH // LANES, tm, LANES), lambda i: (0, i, 0))
    return pl.pallas_call(
        kern,
        grid=(m // tm,),
        in_specs=[
            rows(D_MODEL), rows(ATTN_WIDTH), rows(CONV_WIDTH), slab, slab,
            rows(COL, COL_G_LRU), pl.BlockSpec((None, tm, PLE_DIM), lambda i: (l, i, 0)),
            _layer_resident(wo_b, l), _layer_resident(pn, l), _layer_resident(wg_b, l),
            _layer_resident(wp_b, l), _layer_resident(fn, 0),
        ],
        out_specs=rows(D_MODEL),
        out_shape=jax.ShapeDtypeStruct((m, D_MODEL), F32),
        compiler_params=pltpu.CompilerParams(
            dimension_semantics=("arbitrary",), vmem_limit_bytes=VMEM_LIMIT),
        name="outple",
    )(h2d, ya, yc, hf, hb, z2d, p3d, wo_b, pn, wg_b, wp_b, fn)


def _trunk(x, p, prm, bias):
    b, s, _ = x.shape
    m = b * s
    depth = p.shape[0]
    h = x.reshape(m, D_MODEL)
    p3d = p.reshape(depth, m, PLE_DIM)
    for l in range(depth):
        z = _inproj(h, prm["norm_mix"], prm["w_in"], l)
        z3 = z.reshape(b, s, IN_WIDTH)
        ya = _attention(z3, bias, prm["attn_sink"], l)
        yc, hf, hb = _mixers(z3, prm["conv_w"], prm["lru_conv_w"], prm["lru_conv_b"],
                             prm["wai"], prm["lru_b_a"], prm["lru_b_i"], prm["lru_L"], l)
        h = _outple(h, ya.reshape(m, ATTN_WIDTH), yc.reshape(m, CONV_WIDTH),
                    hf.reshape(-1, m, LANES), hb.reshape(-1, m, LANES), z, p3d,
                    prm["w_out"], prm["ple_norm"], prm["ple_w_gate"], prm["ple_w_proj"],
                    prm["final_norm"], l, last=(l == depth - 1))
    return h.reshape(b, s, D_MODEL)


def kernel(x_prompt, x_sample, p_prompt, p_sample, norm_mix, w_in, w_out, rel_bias, attn_sink, conv_w,
           lru_conv_w, lru_conv_b, lru_w_a, lru_b_a, lru_w_i, lru_b_i, lru_L, ple_norm, ple_w_gate,
           ple_w_proj, final_norm):
    prm = dict(
        norm_mix=norm_mix[:, None, :], w_in=w_in.astype(BF16), w_out=w_out.astype(BF16), attn_sink=attn_sink,
        conv_w=conv_w, lru_conv_w=lru_conv_w, lru_conv_b=lru_conv_b,
        wai=jnp.concatenate([lru_w_a, lru_w_i], axis=-1).astype(BF16),
        lru_b_a=lru_b_a, lru_b_i=lru_b_i, lru_L=lru_L, ple_norm=ple_norm[:, None, :],
        ple_w_gate=ple_w_gate.astype(BF16), ple_w_proj=ple_w_proj.astype(BF16),
        final_norm=final_norm[None, None, :])
    bias = _rel_bias(rel_bias)
    y_prompt = _trunk(x_prompt, p_prompt, prm, bias)
    y_sample = _trunk(x_sample, p_sample, prm, bias)
    return (y_prompt, y_sample)
```

```python
import functools
import math

import numpy as np
import jax
import jax.numpy as jnp
from jax import lax
from jax.experimental import pallas as pl
from jax.experimental.pallas import tpu as pltpu

F32 = jnp.float32
BF16 = jnp.bfloat16

D_MODEL = 2048
N_HEADS = 8
N_KV_HEADS = 2
Q_GROUP = N_HEADS // N_KV_HEADS
HEAD_DIM = 128
ATTN_WIDTH = N_HEADS * HEAD_DIM
KV_WIDTH = N_KV_HEADS * HEAD_DIM
WINDOW = 128
BLOCK = 128
N_BUCKETS = 32
MAX_DISTANCE = 128
CONV_WIDTH = 512
CONV_K = 3
LRU_WIDTH = 512
LRU_HEADS = 4
LRU_BLOCK = 128
LRU_CONV_K = 4
LRU_C = 8.0
MIX_WIDTH = ATTN_WIDTH + CONV_WIDTH + LRU_WIDTH
IN_WIDTH = 2 * ATTN_WIDTH + 2 * KV_WIDTH + 4 * CONV_WIDTH + 2 * LRU_WIDTH
PLE_DIM = 256
NORM_EPS = 1e-6
NEG_INF = -1e30
LOG2E = math.log2(math.e)
Q_SCALE = HEAD_DIM ** -0.5 * LOG2E

COL = 512
COL_KV = 2
COL_G_ATTN = 3
COL_CONV_B = 5
COL_CONV_C = 6
COL_CONV_X = 7
COL_G_CONV = 8
COL_LRU_X = 9
COL_G_LRU = 10

SUBLANES = 8
LANES = 128
VMEM_LIMIT = 60 * 1024 * 1024

INPROJ_TM = 512
INPROJ_TN = 512
ATTN_QB = 4
MIX_T = 512
MIX_CHUNK = 8
OUT_TM = 512
OUT_SUB = 256


def _sigmoid(x):
    return 1.0 / (1.0 + jnp.exp2(x * (-LOG2E)))


def _silu(x):
    return x * _sigmoid(x)


def _rms(x, g):
    ms = jnp.mean(x * x, axis=-1, keepdims=True)
    return x * lax.rsqrt(ms + NORM_EPS) * g


def _layer_resident(x, l):
    nd = x.ndim - 1
    return pl.BlockSpec((None,) + x.shape[1:], lambda *_: (l,) + (0,) * nd, pipeline_mode=pl.Buffered(1))


def _inproj_kernel(h_ref, g_ref, w_ref, z_ref):
    u = _rms(h_ref[...], g_ref[...]).astype(BF16)
    for c in range(IN_WIDTH // INPROJ_TN):
        cols = slice(c * INPROJ_TN, (c + 1) * INPROJ_TN)
        z_ref[:, cols] = jnp.dot(u, w_ref[:, cols], preferred_element_type=F32).astype(z_ref.dtype)


def _inproj(h2d, g, w_b, l):
    m = h2d.shape[0]
    tm = INPROJ_TM
    return pl.pallas_call(
        _inproj_kernel,
        grid=(m // tm,),
        in_specs=[
            pl.BlockSpec((tm, D_MODEL), lambda i: (i, 0)),
            _layer_resident(g, l),
            _layer_resident(w_b, l),
        ],
        out_specs=pl.BlockSpec((tm, IN_WIDTH), lambda i: (i, 0)),
        out_shape=jax.ShapeDtypeStruct((m, IN_WIDTH), BF16),
        compiler_params=pltpu.CompilerParams(
            dimension_semantics=("arbitrary",), vmem_limit_bytes=VMEM_LIMIT),
        name="inproj",
    )(h2d, g, w_b)


def _bucket_table():
    q = np.arange(BLOCK)[:, None]
    c = np.arange(3 * BLOCK)[None, :]
    rel = c - BLOCK - q
    n = np.abs(rel)
    half = N_BUCKETS // 2
    max_exact = half // 2
    n2 = np.maximum(n, 1).astype(np.int64) ** 2
    floor_log2 = np.floor(np.log2(n2.astype(np.float64)) + 1e-9).astype(np.int64)
    large = np.minimum(max_exact + floor_log2 - 6, half - 1)
    bucket = np.where(rel > 0, half, 0) + np.where(n < max_exact, n, large)
    bucket = np.where(n <= WINDOW, bucket, -1)
    return bucket.astype(np.int32)


BIAS_INTERIOR, BIAS_FIRST, BIAS_LAST = 0, 1, 2


def _bias_kernel(tab_ref, bkt_ref, o_ref):
    v = pl.program_id(0)
    h = pl.program_id(1)
    b = bkt_ref[...]
    acc = jnp.full(b.shape, NEG_INF, F32)
    for k in range(N_BUCKETS):
        acc = jnp.where(b == k, tab_ref[k, h] * LOG2E, acc)
    col = lax.broadcasted_iota(jnp.int32, b.shape, 1)
    outside = ((col < BLOCK) & (v == BIAS_FIRST)) | ((col >= 2 * BLOCK) & (v == BIAS_LAST))
    o_ref[0, 0] = jnp.where(outside, NEG_INF, acc)


def _rel_bias(rel_table):
    bkt = jnp.asarray(_bucket_table())
    return pl.pallas_call(
        _bias_kernel,
        grid=(3, N_HEADS),
        in_specs=[
            pl.BlockSpec(memory_space=pltpu.SMEM),
            pl.BlockSpec((BLOCK, 3 * BLOCK), lambda v, h: (0, 0)),
        ],
        out_specs=pl.BlockSpec((1, 1, BLOCK, 3 * BLOCK), lambda v, h: (v, h, 0, 0)),
        out_shape=jax.ShapeDtypeStruct((3, N_HEADS, BLOCK, 3 * BLOCK), F32),
        name="relbias",
    )(rel_table, bkt)


def _attn_kernel(sink_ref, q_ref, kvp_ref, kvc_ref, kvn_ref, bias_ref, o_ref, *, nblk, layer):
    i = pl.program_id(1)
    ones = jnp.ones((3 * BLOCK, HEAD_DIM), BF16)
    for kh in range(N_KV_HEADS):
        ks = slice(kh * HEAD_DIM, (kh + 1) * HEAD_DIM)
        vs = slice(KV_WIDTH + kh * HEAD_DIM, KV_WIDTH + (kh + 1) * HEAD_DIM)
        k_win = jnp.concatenate([kvp_ref[:, ks], kvc_ref[:, ks], kvn_ref[:, ks]], axis=0)
        v_win = jnp.concatenate([kvp_ref[:, vs], kvc_ref[:, vs], kvn_ref[:, vs]], axis=0)
        for s in range(ATTN_QB):
            n = i * ATTN_QB + s
            variant = jnp.where(n == 0, BIAS_FIRST, jnp.where(n == nblk - 1, BIAS_LAST, BIAS_INTERIOR))
            bias4 = bias_ref[variant, kh * Q_GROUP:(kh + 1) * Q_GROUP].reshape(Q_GROUP * BLOCK, 3 * BLOCK)
            rows = slice(s * BLOCK, (s + 1) * BLOCK)
            q4 = jnp.concatenate(
                [q_ref[rows, (kh * Q_GROUP + g) * HEAD_DIM:(kh * Q_GROUP + g + 1) * HEAD_DIM]
                 for g in range(Q_GROUP)], axis=0)
            k_sub = k_win[s * BLOCK:s * BLOCK + 3 * BLOCK]
            v_ext = jnp.concatenate([v_win[s * BLOCK:s * BLOCK + 3 * BLOCK], ones], axis=1)
            sc = lax.dot_general(q4, k_sub, (((1,), (1,)), ((), ())), preferred_element_type=F32) + bias4
            es, sink_terms = [], []
            for g in range(Q_GROUP):
                sg = sc[g * BLOCK:(g + 1) * BLOCK]
                sink2 = sink_ref[layer, kh * Q_GROUP + g] * LOG2E
                m = jnp.maximum(jnp.max(sg, axis=-1, keepdims=True), sink2)
                es.append(jnp.exp2(sg - m).astype(BF16))
                sink_terms.append(jnp.exp2(sink2 - m))
            pv = jnp.dot(jnp.concatenate(es, axis=0), v_ext, preferred_element_type=F32)
            for g in range(Q_GROUP):
                hd = kh * Q_GROUP + g
                pg = pv[g * BLOCK:(g + 1) * BLOCK]
                o = pg[:, :HEAD_DIM] / (pg[:, HEAD_DIM:] + sink_terms[g])
                o_ref[rows, hd * HEAD_DIM:(hd + 1) * HEAD_DIM] = o.astype(o_ref.dtype)


def _attention(z3, bias, sink, layer):
    b, s, _ = z3.shape
    nblk = s // BLOCK
    assert nblk >= 2 and nblk % ATTN_QB == 0
    tq = ATTN_QB * BLOCK
    kern = functools.partial(_attn_kernel, nblk=nblk, layer=layer)
    return pl.pallas_call(
        kern,
        grid=(b, s // tq),
        in_specs=[
            pl.BlockSpec(memory_space=pltpu.SMEM),
            pl.BlockSpec((None, tq, ATTN_WIDTH), lambda bi, i: (bi, i, 0)),
            pl.BlockSpec((None, BLOCK, COL), lambda bi, i: (bi, jnp.maximum(i * ATTN_QB - 1, 0), COL_KV)),
            pl.BlockSpec((None, tq, COL), lambda bi, i: (bi, i, COL_KV)),
            pl.BlockSpec((None, BLOCK, COL),
                         lambda bi, i: (bi, jnp.minimum((i + 1) * ATTN_QB, nblk - 1), COL_KV)),
            pl.BlockSpec(bias.shape, lambda bi, i: (0, 0, 0, 0)),
        ],
        out_specs=pl.BlockSpec((None, tq, ATTN_WIDTH), lambda bi, i: (bi, i, 0)),
        out_shape=jax.ShapeDtypeStruct((b, s, ATTN_WIDTH), BF16),
        compiler_params=pltpu.CompilerParams(
            dimension_semantics=("arbitrary", "arbitrary"), vmem_limit_bytes=VMEM_LIMIT),
        name="attn",
    )(sink, z3, z3, z3, z3, bias)


def _segment_permutation(t):
    groups = t // SUBLANES
    i = np.arange(t)
    p = np.zeros((t, t), np.float32)
    p[i, (i % SUBLANES) * groups + i // SUBLANES] = 1.0
    return p


def _lru_gates(xc, log_a_scale, wai_ref, ba_ref, bi_ref, d):
    ra, ri = [], []
    for hd in range(LRU_HEADS):
        xh = xc[:, hd * LRU_BLOCK:(hd + 1) * LRU_BLOCK].astype(BF16)
        y = jnp.dot(xh, wai_ref[d, hd], preferred_element_type=F32)
        ra.append(y[:, :LRU_BLOCK])
        ri.append(y[:, LRU_BLOCK:])
    r = _sigmoid(jnp.concatenate(ra, axis=1) + ba_ref[d:d + 1, :])
    gi = _sigmoid(jnp.concatenate(ri, axis=1) + bi_ref[d:d + 1, :])
    log_a = log_a_scale * r
    a = jnp.exp(log_a)
    v = -jnp.tanh(log_a) * (a * a + 1.0)
    root = jnp.where(v > 0.0, v * lax.rsqrt(v), 0.0)
    return a, root * (gi * xc)


def _lru_direction(x_ref, perm_ref, halo, carry, a_scr, u_scr, out_ref,
                   lw_ref, lb_ref, wai_ref, ba_ref, bi_ref, lam_ref, d):
    reverse = d == 1
    t = MIX_T
    groups = t // SUBLANES
    w = LRU_WIDTH
    k = LRU_CONV_K - 1
    nl = -lam_ref[d:d + 1, :]
    log_a_scale = -LRU_C * (jnp.maximum(nl, 0.0) + jnp.log1p(jnp.exp(-jnp.abs(nl))))
    x3 = jnp.dot(perm_ref[...], x_ref[...], preferred_element_type=F32).reshape(groups, SUBLANES, w)
    row = lax.broadcasted_iota(jnp.int32, (1, SUBLANES, w), 1)
    if reverse:
        src = x3[:k]
        wrapped = pltpu.roll(jnp.where(row == 0, halo[...], src), SUBLANES - 1, axis=1)
        ext = jnp.concatenate([x3, wrapped], axis=0)
    else:
        src = x3[groups - k:]
        wrapped = pltpu.roll(jnp.where(row == SUBLANES - 1, halo[...], src), 1, axis=1)
        ext = jnp.concatenate([wrapped, x3], axis=0)
    halo[...] = src
    taps = [lw_ref[d, j:j + 1, :] for j in range(LRU_CONV_K)]
    bias = lb_ref[d:d + 1, :]

    def in_time_order(n):
        return range(n - 1, -1, -1) if reverse else range(n)

    h = p = None
    for ci in in_time_order(groups // MIX_CHUNK):
        g0 = ci * MIX_CHUNK
        xc = bias + sum(taps[j] * ext[g0 + j:g0 + j + MIX_CHUNK] for j in range(LRU_CONV_K))
        a, u = _lru_gates(xc.reshape(MIX_CHUNK * SUBLANES, w), log_a_scale, wai_ref, ba_ref, bi_ref, d)
        a3 = a.reshape(MIX_CHUNK, SUBLANES, w)
        u3 = u.reshape(MIX_CHUNK, SUBLANES, w)
        a_scr[g0:g0 + MIX_CHUNK] = a3
        u_scr[g0:g0 + MIX_CHUNK] = u3
        for g in in_time_order(MIX_CHUNK):
            h = u3[g] if h is None else a3[g] * h + u3[g]
            p = a3[g] if p is None else a3[g] * p

    c = carry[0:1, :]
    cs = [None] * SUBLANES
    for r in in_time_order(SUBLANES):
        cs[r] = c
        c = p[r:r + 1, :] * c + h[r:r + 1, :]
    carry[0:1, :] = c

    h = jnp.concatenate(cs, axis=0)
    for g in in_time_order(groups):
        h = a_scr[g] * h + u_scr[g]
        for j in range(w // LANES):
            out_ref[j, pl.ds(g, SUBLANES, stride=groups), :] = h[:, j * LANES:(j + 1) * LANES]


def _mixers_kernel(cc_ref, cx_ref, xf_ref, xb_ref, ccn_ref, cxn_ref, perm_ref,
                   cw_ref, lw_ref, lb_ref, wai_ref, ba_ref, bi_ref, lam_ref,
                   yc_ref, hf_ref, hb_ref,
                   bufc, halo_f, halo_b, carry_f, carry_b, a_f, u_f, a_b, u_b, *, nt):
    i = pl.program_id(1)
    t = MIX_T
    pad = SUBLANES

    @pl.when(i == 0)
    def _():
        bufc[0:pad, :] = jnp.zeros((pad, CONV_WIDTH), F32)
        halo_f[...] = jnp.zeros(halo_f.shape, F32)
        halo_b[...] = jnp.zeros(halo_b.shape, F32)
        carry_f[...] = jnp.zeros(carry_f.shape, F32)
        carry_b[...] = jnp.zeros(carry_b.shape, F32)

    rows_per_chunk = MIX_CHUNK * SUBLANES
    for r0 in range(0, t, rows_per_chunk):
        rows = slice(r0, r0 + rows_per_chunk)
        bufc[pad + r0:pad + r0 + rows_per_chunk, :] = cc_ref[rows, :].astype(F32) * cx_ref[rows, :].astype(F32)
    nxt = ccn_ref[0:pad, :].astype(F32) * cxn_ref[0:pad, :].astype(F32)
    bufc[pad + t:pad + t + pad, :] = jnp.where(i == nt - 1, 0.0, nxt)
    for r0 in range(0, t, rows_per_chunk):
        conv = sum(cw_ref[j:j + 1, :] * bufc[pad - 1 + j + r0:pad - 1 + j + r0 + rows_per_chunk, :]
                   for j in range(CONV_K))
        yc_ref[r0:r0 + rows_per_chunk, :] = conv.astype(yc_ref.dtype)
    bufc[0:pad, :] = bufc[t:t + pad, :]

    lru = (lw_ref, lb_ref, wai_ref, ba_ref, bi_ref, lam_ref)
    _lru_direction(xf_ref, perm_ref, halo_f, carry_f, a_f, u_f, hf_ref, *lru, 0)
    _lru_direction(xb_ref, perm_ref, halo_b, carry_b, a_b, u_b, hb_ref, *lru, 1)


def _mixers(z3, conv_w, lru_conv_w, lru_conv_b, wai, b_a, b_i, lam, l):
    b, s, _ = z3.shape
    t = MIX_T
    nt = s // t
    halo = 16
    slabs = LRU_WIDTH // LANES
    kern = functools.partial(_mixers_kernel, nt=nt)
    perm = jnp.asarray(_segment_permutation(t), BF16)

    def zcol(c):
        return pl.BlockSpec((None, t, COL), lambda bi, i: (bi, i, c))

    def znext(c):
        return pl.BlockSpec((None, halo, COL),
                            lambda bi, i: (bi, jnp.minimum((i + 1) * (t // halo), s // halo - 1), c))

    params = (conv_w, lru_conv_w, lru_conv_b, wai, b_a, b_i, lam)
    return pl.pallas_call(
        kern,
        grid=(b, nt),
        in_specs=[
            zcol(COL_CONV_C), zcol(COL_CONV_X), zcol(COL_LRU_X),
            pl.BlockSpec((None, t, COL), lambda bi, i: (bi, nt - 1 - i, COL_LRU_X)),
            znext(COL_CONV_C), znext(COL_CONV_X),
            pl.BlockSpec((t, t), lambda bi, i: (0, 0)),
        ] + [_layer_resident(x, l) for x in params],
        out_specs=[
            pl.BlockSpec((None, t, CONV_WIDTH), lambda bi, i: (bi, i, 0)),
            pl.BlockSpec((slabs, None, t, LANES), lambda bi, i: (0, bi, i, 0)),
            pl.BlockSpec((slabs, None, t, LANES), lambda bi, i: (0, bi, nt - 1 - i, 0)),
        ],
        out_shape=[
            jax.ShapeDtypeStruct((b, s, CONV_WIDTH), BF16),
            jax.ShapeDtypeStruct((slabs, b, s, LANES), F32),
            jax.ShapeDtypeStruct((slabs, b, s, LANES), F32),
        ],
        scratch_shapes=[
            pltpu.VMEM((t + 2 * SUBLANES, CONV_WIDTH), F32),
            pltpu.VMEM((LRU_CONV_K - 1, SUBLANES, LRU_WIDTH), F32),
            pltpu.VMEM((LRU_CONV_K - 1, SUBLANES, LRU_WIDTH), F32),
            pltpu.VMEM((SUBLANES, LRU_WIDTH), F32),
            pltpu.VMEM((SUBLANES, LRU_WIDTH), F32),
        ] + [pltpu.VMEM((t // SUBLANES, SUBLANES, LRU_WIDTH), F32)] * 4,
        compiler_params=pltpu.CompilerParams(
            dimension_semantics=("arbitrary", "arbitrary"), vmem_limit_bytes=VMEM_LIMIT),
        name="mixers",
    )(z3, z3, z3, z3, z3, z3, perm, *params)


def _outple_kernel(h_ref, ya_ref, yc_ref, hf_ref, hb_ref, ga0_ref, ga1_ref, cb_ref, gc_ref, gl_ref, p_ref,
                   wo_ref, pn_ref, wg_ref, wp_ref, fn_ref, o_ref, *, last):
    for s in range(OUT_TM // OUT_SUB):
        rows = slice(s * OUT_SUB, (s + 1) * OUT_SUB)
        g_attn = jnp.concatenate([ga0_ref[rows, :], ga1_ref[rows, :]], axis=1).astype(F32)
        y_attn = ya_ref[rows, :].astype(F32) * _silu(g_attn)
        y_conv = cb_ref[rows, :].astype(F32) * yc_ref[rows, :].astype(F32) * _silu(gc_ref[rows, :].astype(F32))
        hsum = hf_ref[:, rows, :] + hb_ref[:, rows, :]
        y_lru = jnp.concatenate([hsum[j] for j in range(LRU_WIDTH // LANES)], axis=1)
        y_lru = y_lru * _silu(gl_ref[rows, :].astype(F32))
        mix = jnp.concatenate([y_attn.astype(BF16), y_conv.astype(BF16), y_lru.astype(BF16)], axis=1)
        h1 = h_ref[rows, :] + jnp.dot(mix, wo_ref[...], preferred_element_type=F32)
        n = _rms(h1, pn_ref[...]).astype(BF16)
        gate = _sigmoid(jnp.dot(n, wg_ref[...], preferred_element_type=F32))
        pp = jnp.dot(p_ref[rows, :].astype(BF16), wp_ref[...], preferred_element_type=F32)
        h2 = h1 + gate * pp
        if last:
            h2 = _rms(h2, fn_ref[...])
        o_ref[rows, :] = h2


def _outple(h2d, ya, yc, hf, hb, z2d, p3d, wo_b, pn, wg_b, wp_b, fn, l, last):
    m = h2d.shape[0]
    tm = OUT_TM
    kern = functools.partial(_outple_kernel, last=last)

    def rows(width, c=0):
        return pl.BlockSpec((tm, width), lambda i: (i, c))

    slab = pl.BlockSpec((LRU_WIDTH // LANES, tm, LANES), lambda i: (0, i, 0))
    return pl.pallas_call(
        kern,
        grid=(m // tm,),
        in_specs=[
            rows(D_MODEL), rows(ATTN_WIDTH), rows(CONV_WIDTH), slab, slab,
            rows(COL, COL_G_ATTN), rows(COL, COL_G_ATTN + 1), rows(COL, COL_CONV_B), rows(COL, COL_G_CONV),
            rows(COL, COL_G_LRU), pl.BlockSpec((None, tm, PLE_DIM), lambda i: (l, i, 0)),
            _layer_resident(wo_b, l), _layer_resident(pn, l), _layer_resident(wg_b, l),
            _layer_resident(wp_b, l), _layer_resident(fn, 0),
        ],
        out_specs=rows(D_MODEL),
        out_shape=jax.ShapeDtypeStruct((m, D_MODEL), F32),
        compiler_params=pltpu.CompilerParams(
            dimension_semantics=("arbitrary",), vmem_limit_bytes=VMEM_LIMIT),
        name="outple",
    )(h2d, ya, yc, hf, hb, z2d, z2d, z2d, z2d, z2d, p3d, wo_b, pn, wg_b, wp_b, fn)


def _trunk(x, p, prm, bias):
    b, s, _ = x.shape
    m = b * s
    depth = p.shape[0]
    h = x.reshape(m, D_MODEL)
    p3d = p.reshape(depth, m, PLE_DIM)
    for l in range(depth):
        z = _inproj(h, prm["norm_mix"], prm["w_in"], l)
        z3 = z.reshape(b, s, IN_WIDTH)
        ya = _attention(z3, bias, prm["attn_sink"], l)
        yc, hf, hb = _mixers(z3, prm["conv_w"], prm["lru_conv_w"], prm["lru_conv_b"],
                             prm["wai"], prm["lru_b_a"], prm["lru_b_i"], prm["lru_L"], l)
        h = _outple(h, ya.reshape(m, ATTN_WIDTH), yc.reshape(m, CONV_WIDTH),
                    hf.reshape(-1, m, LANES), hb.reshape(-1, m, LANES), z, p3d,
                    prm["w_out"], prm["ple_norm"], prm["ple_w_gate"], prm["ple_w_proj"],
                    prm["final_norm"], l, last=(l == depth - 1))
    return h.reshape(b, s, D_MODEL)


def kernel(x_prompt, x_sample, p_prompt, p_sample, norm_mix, w_in, w_out, rel_bias, attn_sink, conv_w,
           lru_conv_w, lru_conv_b, lru_w_a, lru_b_a, lru_w_i, lru_b_i, lru_L, ple_norm, ple_w_gate,
           ple_w_proj, final_norm):
    col_scale = jnp.where(jnp.arange(IN_WIDTH) < ATTN_WIDTH, Q_SCALE, 1.0).astype(F32)
    prm = dict(
        norm_mix=norm_mix[:, None, :], w_in=(w_in * col_scale).astype(BF16), w_out=w_out.astype(BF16),
        attn_sink=attn_sink, conv_w=conv_w, lru_conv_w=lru_conv_w, lru_conv_b=lru_conv_b,
        wai=jnp.concatenate([lru_w_a, lru_w_i], axis=-1).astype(BF16),
        lru_b_a=lru_b_a, lru_b_i=lru_b_i, lru_L=lru_L, ple_norm=ple_norm[:, None, :],
        ple_w_gate=ple_w_gate.astype(BF16), ple_w_proj=ple_w_proj.astype(BF16),
        final_norm=final_norm[None, None, :])
    bias = _rel_bias(rel_bias)
    y_prompt = _trunk(x_prompt, p_prompt, prm, bias)
    y_sample = _trunk(x_sample, p_sample, prm, bias)
    return (y_prompt, y_sample)
```

```python
import functools
import math

import numpy as np
import jax
import jax.numpy as jnp
from jax import lax
from jax.experimental import pallas as pl
from jax.experimental.pallas import tpu as pltpu

F32 = jnp.float32
BF16 = jnp.bfloat16

D_MODEL = 2048
N_HEADS = 8
N_KV_HEADS = 2
Q_GROUP = N_HEADS // N_KV_HEADS
HEAD_DIM = 128
ATTN_WIDTH = N_HEADS * HEAD_DIM
KV_WIDTH = N_KV_HEADS * HEAD_DIM
WINDOW = 128
BLOCK = 128
N_BUCKETS = 32
MAX_DISTANCE = 128
CONV_WIDTH = 512
CONV_K = 3
LRU_WIDTH = 512
LRU_HEADS = 4
LRU_BLOCK = 128
LRU_CONV_K = 4
LRU_C = 8.0
MIX_WIDTH = ATTN_WIDTH + CONV_WIDTH + LRU_WIDTH
IN_WIDTH = 2 * ATTN_WIDTH + 2 * KV_WIDTH + 4 * CONV_WIDTH + 2 * LRU_WIDTH
PLE_DIM = 256
NORM_EPS = 1e-6
NEG_INF = -1e30
LOG2E = math.log2(math.e)
Q_SCALE = HEAD_DIM ** -0.5 * LOG2E

COL = 512
COL_KV = 2
COL_G_ATTN = 3
COL_CONV_B = 5
COL_CONV_C = 6
COL_CONV_X = 7
COL_G_CONV = 8
COL_LRU_X = 9
COL_G_LRU = 10

SUBLANES = 8
LANES = 128
VMEM_LIMIT = 60 * 1024 * 1024

INPROJ_TM = 512
INPROJ_TN = 512
ATTN_QB = 4
MIX_T = 512
MIX_CHUNK = 8
OUT_TM = 512
OUT_SUB = 256


def _sigmoid(x):
    return 1.0 / (1.0 + jnp.exp2(x * (-LOG2E)))


def _silu(x):
    return x * _sigmoid(x)


def _rms(x, g):
    ms = jnp.mean(x * x, axis=-1, keepdims=True)
    return x * lax.rsqrt(ms + NORM_EPS) * g


def _layer_resident(x, l):
    nd = x.ndim - 1
    return pl.BlockSpec((None,) + x.shape[1:], lambda *_: (l,) + (0,) * nd, pipeline_mode=pl.Buffered(1))


def _inproj_kernel(h_ref, g_ref, w_ref, z_ref):
    u = _rms(h_ref[...], g_ref[...]).astype(BF16)
    for c in range(IN_WIDTH // INPROJ_TN):
        cols = slice(c * INPROJ_TN, (c + 1) * INPROJ_TN)
        z_ref[:, cols] = jnp.dot(u, w_ref[:, cols], preferred_element_type=F32).astype(z_ref.dtype)


def _inproj(h2d, g, w_b, l):
    m = h2d.shape[0]
    tm = INPROJ_TM
    return pl.pallas_call(
        _inproj_kernel,
        grid=(m // tm,),
        in_specs=[
            pl.BlockSpec((tm, D_MODEL), lambda i: (i, 0)),
            _layer_resident(g, l),
            _layer_resident(w_b, l),
        ],
        out_specs=pl.BlockSpec((tm, IN_WIDTH), lambda i: (i, 0)),
        out_shape=jax.ShapeDtypeStruct((m, IN_WIDTH), BF16),
        compiler_params=pltpu.CompilerParams(
            dimension_semantics=("arbitrary",), vmem_limit_bytes=VMEM_LIMIT),
        name="inproj",
    )(h2d, g, w_b)


def _bucket_table():
    q = np.arange(BLOCK)[:, None]
    c = np.arange(3 * BLOCK)[None, :]
    rel = c - BLOCK - q
    n = np.abs(rel)
    half = N_BUCKETS // 2
    max_exact = half // 2
    n2 = np.maximum(n, 1).astype(np.int64) ** 2
    floor_log2 = np.floor(np.log2(n2.astype(np.float64)) + 1e-9).astype(np.int64)
    large = np.minimum(max_exact + floor_log2 - 6, half - 1)
    bucket = np.where(rel > 0, half, 0) + np.where(n < max_exact, n, large)
    bucket = np.where(n <= WINDOW, bucket, -1)
    return bucket.astype(np.int32)


BIAS_INTERIOR, BIAS_FIRST, BIAS_LAST = 0, 1, 2


def _bias_kernel(tab_ref, bkt_ref, o_ref):
    h = pl.program_id(0)
    b = bkt_ref[...]
    acc = jnp.full(b.shape, NEG_INF, F32)
    for k in range(N_BUCKETS):
        acc = jnp.where(b == k, tab_ref[k, h] * LOG2E, acc)
    col = lax.broadcasted_iota(jnp.int32, b.shape, 1)
    o_ref[BIAS_INTERIOR, 0] = acc
    o_ref[BIAS_FIRST, 0] = jnp.where(col < BLOCK, NEG_INF, acc)
    o_ref[BIAS_LAST, 0] = jnp.where(col >= 2 * BLOCK, NEG_INF, acc)


def _rel_bias(rel_table):
    bkt = jnp.asarray(_bucket_table())
    return pl.pallas_call(
        _bias_kernel,
        grid=(N_HEADS,),
        in_specs=[
            pl.BlockSpec(memory_space=pltpu.SMEM),
            pl.BlockSpec((BLOCK, 3 * BLOCK), lambda h: (0, 0)),
        ],
        out_specs=pl.BlockSpec((3, 1, BLOCK, 3 * BLOCK), lambda h: (0, h, 0, 0)),
        out_shape=jax.ShapeDtypeStruct((3, N_HEADS, BLOCK, 3 * BLOCK), F32),
        name="relbias",
    )(rel_table, bkt)


def _attn_kernel(sink_ref, q_ref, kvp_ref, kvc_ref, kvn_ref, bias_ref, o_ref, *, nblk, layer):
    i = pl.program_id(1)
    windows = []
    for kh in range(N_KV_HEADS):
        ks = slice(kh * HEAD_DIM, (kh + 1) * HEAD_DIM)
        vs = slice(KV_WIDTH + kh * HEAD_DIM, KV_WIDTH + (kh + 1) * HEAD_DIM)
        windows.append((jnp.concatenate([kvp_ref[:, ks], kvc_ref[:, ks], kvn_ref[:, ks]], axis=0),
                        jnp.concatenate([kvp_ref[:, vs], kvc_ref[:, vs], kvn_ref[:, vs]], axis=0)))
    for kh, (k_win, v_win) in enumerate(windows):
        for s in range(ATTN_QB):
            n = i * ATTN_QB + s
            variant = jnp.where(n == 0, BIAS_FIRST, jnp.where(n == nblk - 1, BIAS_LAST, BIAS_INTERIOR))
            bias4 = bias_ref[variant, kh * Q_GROUP:(kh + 1) * Q_GROUP].reshape(Q_GROUP * BLOCK, 3 * BLOCK)
            rows = slice(s * BLOCK, (s + 1) * BLOCK)
            q4 = jnp.concatenate(
                [q_ref[rows, (kh * Q_GROUP + g) * HEAD_DIM:(kh * Q_GROUP + g + 1) * HEAD_DIM]
                 for g in range(Q_GROUP)], axis=0)
            k_sub = k_win[s * BLOCK:s * BLOCK + 3 * BLOCK]
            v_sub = v_win[s * BLOCK:s * BLOCK + 3 * BLOCK]
            sc = lax.dot_general(q4, k_sub, (((1,), (1,)), ((), ())), preferred_element_type=F32) + bias4
            es, dens = [], []
            for g in range(Q_GROUP):
                sg = sc[g * BLOCK:(g + 1) * BLOCK]
                sink2 = sink_ref[layer, kh * Q_GROUP + g] * LOG2E
                m = jnp.maximum(jnp.max(sg, axis=-1, keepdims=True), sink2)
                e = jnp.exp2(sg - m)
                dens.append(jnp.sum(e, axis=-1, keepdims=True) + jnp.exp2(sink2 - m))
                es.append(e.astype(BF16))
            pv = jnp.dot(jnp.concatenate(es, axis=0), v_sub, preferred_element_type=F32)
            for g in range(Q_GROUP):
                hd = kh * Q_GROUP + g
                o = pv[g * BLOCK:(g + 1) * BLOCK] / dens[g]
                o_ref[rows, hd * HEAD_DIM:(hd + 1) * HEAD_DIM] = o.astype(o_ref.dtype)


def _attention(z3, bias, sink, layer):
    b, s, _ = z3.shape
    nblk = s // BLOCK
    assert nblk >= 2 and nblk % ATTN_QB == 0
    tq = ATTN_QB * BLOCK
    kern = functools.partial(_attn_kernel, nblk=nblk, layer=layer)
    return pl.pallas_call(
        kern,
        grid=(b, s // tq),
        in_specs=[
            pl.BlockSpec(memory_space=pltpu.SMEM),
            pl.BlockSpec((None, tq, ATTN_WIDTH), lambda bi, i: (bi, i, 0)),
            pl.BlockSpec((None, BLOCK, COL), lambda bi, i: (bi, jnp.maximum(i * ATTN_QB - 1, 0), COL_KV)),
            pl.BlockSpec((None, tq, COL), lambda bi, i: (bi, i, COL_KV)),
            pl.BlockSpec((None, BLOCK, COL),
                         lambda bi, i: (bi, jnp.minimum((i + 1) * ATTN_QB, nblk - 1), COL_KV)),
            pl.BlockSpec(bias.shape, lambda bi, i: (0, 0, 0, 0)),
        ],
        out_specs=pl.BlockSpec((None, tq, ATTN_WIDTH), lambda bi, i: (bi, i, 0)),
        out_shape=jax.ShapeDtypeStruct((b, s, ATTN_WIDTH), BF16),
        compiler_params=pltpu.CompilerParams(
            dimension_semantics=("arbitrary", "arbitrary"), vmem_limit_bytes=VMEM_LIMIT),
        name="attn",
    )(sink, z3, z3, z3, z3, bias)


def _segment_permutation(t):
    groups = t // SUBLANES
    i = np.arange(t)
    p = np.zeros((t, t), np.float32)
    p[i, (i % SUBLANES) * groups + i // SUBLANES] = 1.0
    return p


def _lru_gates(xc, log_a_scale, wai_ref, ba_ref, bi_ref, d):
    ra, ri = [], []
    for hd in range(LRU_HEADS):
        xh = xc[:, hd * LRU_BLOCK:(hd + 1) * LRU_BLOCK].astype(BF16)
        y = jnp.dot(xh, wai_ref[d, hd], preferred_element_type=F32)
        ra.append(y[:, :LRU_BLOCK])
        ri.append(y[:, LRU_BLOCK:])
    r = _sigmoid(jnp.concatenate(ra, axis=1) + ba_ref[d:d + 1, :])
    gi = _sigmoid(jnp.concatenate(ri, axis=1) + bi_ref[d:d + 1, :])
    log_a = log_a_scale * r
    a = jnp.exp(log_a)
    v = -jnp.tanh(log_a) * (a * a + 1.0)
    root = jnp.where(v > 0.0, v * lax.rsqrt(v), 0.0)
    return a, root * (gi * xc)


def _lru_direction(x_ref, perm_ref, halo, carry, a_scr, u_scr, out_ref,
                   lw_ref, lb_ref, wai_ref, ba_ref, bi_ref, lam_ref, d):
    reverse = d == 1
    t = MIX_T
    groups = t // SUBLANES
    w = LRU_WIDTH
    k = LRU_CONV_K - 1
    nl = -lam_ref[d:d + 1, :]
    log_a_scale = -LRU_C * (jnp.maximum(nl, 0.0) + jnp.log1p(jnp.exp(-jnp.abs(nl))))
    x3 = jnp.dot(perm_ref[...], x_ref[...], preferred_element_type=F32).reshape(groups, SUBLANES, w)
    row = lax.broadcasted_iota(jnp.int32, (1, SUBLANES, w), 1)
    if reverse:
        src = x3[:k]
        wrapped = pltpu.roll(jnp.where(row == 0, halo[...], src), SUBLANES - 1, axis=1)
        ext = jnp.concatenate([x3, wrapped], axis=0)
    else:
        src = x3[groups - k:]
        wrapped = pltpu.roll(jnp.where(row == SUBLANES - 1, halo[...], src), 1, axis=1)
        ext = jnp.concatenate([wrapped, x3], axis=0)
    halo[...] = src
    taps = [lw_ref[d, j:j + 1, :] for j in range(LRU_CONV_K)]
    bias = lb_ref[d:d + 1, :]

    def in_time_order(n):
        return range(n - 1, -1, -1) if reverse else range(n)

    h = p = None
    for ci in in_time_order(groups // MIX_CHUNK):
        g0 = ci * MIX_CHUNK
        xc = bias + sum(taps[j] * ext[g0 + j:g0 + j + MIX_CHUNK] for j in range(LRU_CONV_K))
        a, u = _lru_gates(xc.reshape(MIX_CHUNK * SUBLANES, w), log_a_scale, wai_ref, ba_ref, bi_ref, d)
        a3 = a.reshape(MIX_CHUNK, SUBLANES, w)
        u3 = u.reshape(MIX_CHUNK, SUBLANES, w)
        a_scr[g0:g0 + MIX_CHUNK] = a3
        u_scr[g0:g0 + MIX_CHUNK] = u3
        for g in in_time_order(MIX_CHUNK):
            h = u3[g] if h is None else a3[g] * h + u3[g]
            p = a3[g] if p is None else a3[g] * p

    c = carry[0:1, :]
    cs = [None] * SUBLANES
    for r in in_time_order(SUBLANES):
        cs[r] = c
        c = p[r:r + 1, :] * c + h[r:r + 1, :]
    carry[0:1, :] = c

    h = jnp.concatenate(cs, axis=0)
    for g in in_time_order(groups):
        h = a_scr[g] * h + u_scr[g]
        for j in range(w // LANES):
            out_ref[j, pl.ds(g, SUBLANES, stride=groups), :] = h[:, j * LANES:(j + 1) * LANES]


def _mixers_kernel(cc_ref, cx_ref, xf_ref, xb_ref, ccn_ref, cxn_ref, perm_ref,
                   cw_ref, lw_ref, lb_ref, wai_ref, ba_ref, bi_ref, lam_ref,
                   yc_ref, hf_ref, hb_ref,
                   bufc, halo_f, halo_b, carry_f, carry_b, a_f, u_f, a_b, u_b, *, nt):
    i = pl.program_id(1)
    t = MIX_T
    pad = SUBLANES

    @pl.when(i == 0)
    def _():
        bufc[0:pad, :] = jnp.zeros((pad, CONV_WIDTH), F32)
        halo_f[...] = jnp.zeros(halo_f.shape, F32)
        halo_b[...] = jnp.zeros(halo_b.shape, F32)
        carry_f[...] = jnp.zeros(carry_f.shape, F32)
        carry_b[...] = jnp.zeros(carry_b.shape, F32)

    rows_per_chunk = MIX_CHUNK * SUBLANES
    for r0 in range(0, t, rows_per_chunk):
        rows = slice(r0, r0 + rows_per_chunk)
        bufc[pad + r0:pad + r0 + rows_per_chunk, :] = cc_ref[rows, :].astype(F32) * cx_ref[rows, :].astype(F32)
    nxt = ccn_ref[0:pad, :].astype(F32) * cxn_ref[0:pad, :].astype(F32)
    bufc[pad + t:pad + t + pad, :] = jnp.where(i == nt - 1, 0.0, nxt)
    for r0 in range(0, t, rows_per_chunk):
        conv = sum(cw_ref[j:j + 1, :] * bufc[pad - 1 + j + r0:pad - 1 + j + r0 + rows_per_chunk, :]
                   for j in range(CONV_K))
        yc_ref[r0:r0 + rows_per_chunk, :] = conv.astype(yc_ref.dtype)
    bufc[0:pad, :] = bufc[t:t + pad, :]

    lru = (lw_ref, lb_ref, wai_ref, ba_ref, bi_ref, lam_ref)
    _lru_direction(xf_ref, perm_ref, halo_f, carry_f, a_f, u_f, hf_ref, *lru, 0)
    _lru_direction(xb_ref, perm_ref, halo_b, carry_b, a_b, u_b, hb_ref, *lru, 1)


def _mixers(z3, conv_w, lru_conv_w, lru_conv_b, wai, b_a, b_i, lam, l):
    b, s, _ = z3.shape
    t = MIX_T
    nt = s // t
    halo = 16
    slabs = LRU_WIDTH // LANES
    kern = functools.partial(_mixers_kernel, nt=nt)
    perm = jnp.asarray(_segment_permutation(t), BF16)

    def zcol(c):
        return pl.BlockSpec((None, t, COL), lambda bi, i: (bi, i, c))

    def znext(c):
        return pl.BlockSpec((None, halo, COL),
                            lambda bi, i: (bi, jnp.minimum((i + 1) * (t // halo), s // halo - 1), c))

    params = (conv_w, lru_conv_w, lru_conv_b, wai, b_a, b_i, lam)
    return pl.pallas_call(
        kern,
        grid=(b, nt),
        in_specs=[
            zcol(COL_CONV_C), zcol(COL_CONV_X), zcol(COL_LRU_X),
            pl.BlockSpec((None, t, COL), lambda bi, i: (bi, nt - 1 - i, COL_LRU_X)),
            znext(COL_CONV_C), znext(COL_CONV_X),
            pl.BlockSpec((t, t), lambda bi, i: (0, 0)),
        ] + [_layer_resident(x, l) for x in params],
        out_specs=[
            pl.BlockSpec((None, t, CONV_WIDTH), lambda bi, i: (bi, i, 0)),
            pl.BlockSpec((slabs, None, t, LANES), lambda bi, i: (0, bi, i, 0)),
            pl.BlockSpec((slabs, None, t, LANES), lambda bi, i: (0, bi, nt - 1 - i, 0)),
        ],
        out_shape=[
            jax.ShapeDtypeStruct((b, s, CONV_WIDTH), BF16),
            jax.ShapeDtypeStruct((slabs, b, s, LANES), F32),
            jax.ShapeDtypeStruct((slabs, b, s, LANES), F32),
        ],
        scratch_shapes=[
            pltpu.VMEM((t + 2 * SUBLANES, CONV_WIDTH), F32),
            pltpu.VMEM((LRU_CONV_K - 1, SUBLANES, LRU_WIDTH), F32),
            pltpu.VMEM((LRU_CONV_K - 1, SUBLANES, LRU_WIDTH), F32),
            pltpu.VMEM((SUBLANES, LRU_WIDTH), F32),
            pltpu.VMEM((SUBLANES, LRU_WIDTH), F32),
        ] + [pltpu.VMEM((t // SUBLANES, SUBLANES, LRU_WIDTH), F32)] * 4,
        compiler_params=pltpu.CompilerParams(
            dimension_semantics=("arbitrary", "arbitrary"), vmem_limit_bytes=VMEM_LIMIT),
        name="mixers",
    )(z3, z3, z3, z3, z3, z3, perm, *params)


def _outple_kernel(h_ref, ya_ref, yc_ref, hf_ref, hb_ref, ga0_ref, ga1_ref, cb_ref, gc_ref, gl_ref, p_ref,
                   wo_ref, pn_ref, wg_ref, wp_ref, fn_ref, o_ref, *, last):
    subs = [slice(s * OUT_SUB, (s + 1) * OUT_SUB) for s in range(OUT_TM // OUT_SUB)]
    h1s, pps = [], []
    for rows in subs:
        g_attn = jnp.concatenate([ga0_ref[rows, :], ga1_ref[rows, :]], axis=1).astype(F32)
        y_attn = ya_ref[rows, :].astype(F32) * _silu(g_attn)
        y_conv = cb_ref[rows, :].astype(F32) * yc_ref[rows, :].astype(F32) * _silu(gc_ref[rows, :].astype(F32))
        hsum = hf_ref[:, rows, :] + hb_ref[:, rows, :]
        y_lru = jnp.concatenate([hsum[j] for j in range(LRU_WIDTH // LANES)], axis=1)
        y_lru = y_lru * _silu(gl_ref[rows, :].astype(F32))
        mix = jnp.concatenate([y_attn.astype(BF16), y_conv.astype(BF16), y_lru.astype(BF16)], axis=1)
        h1s.append(h_ref[rows, :] + jnp.dot(mix, wo_ref[...], preferred_element_type=F32))
    for rows in subs:
        pps.append(jnp.dot(p_ref[rows, :].astype(BF16), wp_ref[...], preferred_element_type=F32))
    for rows, h1, pp in zip(subs, h1s, pps):
        n = _rms(h1, pn_ref[...]).astype(BF16)
        gate = _sigmoid(jnp.dot(n, wg_ref[...], preferred_element_type=F32))
        h2 = h1 + gate * pp
        if last:
            h2 = _rms(h2, fn_ref[...])
        o_ref[rows, :] = h2


def _outple(h2d, ya, yc, hf, hb, z2d, p3d, wo_b, pn, wg_b, wp_b, fn, l, last):
    m = h2d.shape[0]
    tm = OUT_TM
    kern = functools.partial(_outple_kernel, last=last)

    def rows(width, c=0):
        return pl.BlockSpec((tm, width), lambda i: (i, c))

    slab = pl.BlockSpec((LRU_WIDTH // LANES, tm, LANES), lambda i: (0, i, 0))
    return pl.pallas_call(
        kern,
        grid=(m // tm,),
        in_specs=[
            rows(D_MODEL), rows(ATTN_WIDTH), rows(CONV_WIDTH), slab, slab,
            rows(COL, COL_G_ATTN), rows(COL, COL_G_ATTN + 1), rows(COL, COL_CONV_B), rows(COL, COL_G_CONV),
            rows(COL, COL_G_LRU), pl.BlockSpec((None, tm, PLE_DIM), lambda i: (l, i, 0)),
            _layer_resident(wo_b, l), _layer_resident(pn, l), _layer_resident(wg_b, l),
            _layer_resident(wp_b, l), _layer_resident(fn, 0),
        ],
        out_specs=rows(D_MODEL),
        out_shape=jax.ShapeDtypeStruct((m, D_MODEL), F32),
        compiler_params=pltpu.CompilerParams(
            dimension_semantics=("arbitrary",), vmem_limit_bytes=VMEM_LIMIT),
        name="outple",
    )(h2d, ya, yc, hf, hb, z2d, z2d, z2d, z2d, z2d, p3d, wo_b, pn, wg_b, wp_b, fn)


def _trunk(x, p, prm, bias):
    b, s, _ = x.shape
    m = b * s
    depth = p.shape[0]
    h = x.reshape(m, D_MODEL)
    p3d = p.reshape(depth, m, PLE_DIM)
    for l in range(depth):
        z = _inproj(h, prm["norm_mix"], prm["w_in"], l)
        z3 = z.reshape(b, s, IN_WIDTH)
        ya = _attention(z3, bias, prm["attn_sink"], l)
        yc, hf, hb = _mixers(z3, prm["conv_w"], prm["lru_conv_w"], prm["lru_conv_b"],
                             prm["wai"], prm["lru_b_a"], prm["lru_b_i"], prm["lru_L"], l)
        h = _outple(h, ya.reshape(m, ATTN_WIDTH), yc.reshape(m, CONV_WIDTH),
                    hf.reshape(-1, m, LANES), hb.reshape(-1, m, LANES), z, p3d,
                    prm["w_out"], prm["ple_norm"], prm["ple_w_gate"], prm["ple_w_proj"],
                    prm["final_norm"], l, last=(l == depth - 1))
    return h.reshape(b, s, D_MODEL)


def kernel(x_prompt, x_sample, p_prompt, p_sample, norm_mix, w_in, w_out, rel_bias, attn_sink, conv_w,
           lru_conv_w, lru_conv_b, lru_w_a, lru_b_a, lru_w_i, lru_b_i, lru_L, ple_norm, ple_w_gate,
           ple_w_proj, final_norm):
    col_scale = jnp.where(jnp.arange(IN_WIDTH) < ATTN_WIDTH, Q_SCALE, 1.0).astype(F32)
    prm = dict(
        norm_mix=norm_mix[:, None, :], w_in=(w_in * col_scale).astype(BF16), w_out=w_out.astype(BF16),
        attn_sink=attn_sink, conv_w=conv_w, lru_conv_w=lru_conv_w, lru_conv_b=lru_conv_b,
        wai=jnp.concatenate([lru_w_a, lru_w_i], axis=-1).astype(BF16),
        lru_b_a=lru_b_a, lru_b_i=lru_b_i, lru_L=lru_L, ple_norm=ple_norm[:, None, :],
        ple_w_gate=ple_w_gate.astype(BF16), ple_w_proj=ple_w_proj.astype(BF16),
        final_norm=final_norm[None, None, :])
    bias = _rel_bias(rel_bias)
    y_prompt = _trunk(x_prompt, p_prompt, prm, bias)
    y_sample = _trunk(x_sample, p_sample, prm, bias)
    return (y_prompt, y_sample)
```

```python
import functools
import math

import numpy as np
import jax
import jax.numpy as jnp
from jax import lax
from jax.experimental import pallas as pl
from jax.experimental.pallas import tpu as pltpu

F32 = jnp.float32
BF16 = jnp.bfloat16

D_MODEL = 2048
N_HEADS = 8
N_KV_HEADS = 2
Q_GROUP = N_HEADS // N_KV_HEADS
HEAD_DIM = 128
ATTN_WIDTH = N_HEADS * HEAD_DIM
KV_WIDTH = N_KV_HEADS * HEAD_DIM
WINDOW = 128
BLOCK = 128
N_BUCKETS = 32
MAX_DISTANCE = 128
CONV_WIDTH = 512
CONV_K = 3
LRU_WIDTH = 512
LRU_HEADS = 4
LRU_BLOCK = 128
LRU_CONV_K = 4
LRU_C = 8.0
MIX_WIDTH = ATTN_WIDTH + CONV_WIDTH + LRU_WIDTH
IN_WIDTH = 2 * ATTN_WIDTH + 2 * KV_WIDTH + 4 * CONV_WIDTH + 2 * LRU_WIDTH
PLE_DIM = 256
NORM_EPS = 1e-6
NEG_INF = -1e30
LOG2E = math.log2(math.e)
Q_SCALE = HEAD_DIM ** -0.5 * LOG2E

COL = 512
COL_KV = 2
COL_G_ATTN = 3
COL_CONV_B = 5
COL_CONV_C = 6
COL_CONV_X = 7
COL_G_CONV = 8
COL_LRU_X = 9
COL_G_LRU = 10

SUBLANES = 8
LANES = 128
VMEM_LIMIT = 60 * 1024 * 1024

INPROJ_TM = 512
INPROJ_TN = 512
ATTN_QB = 4
MIX_T = 512
MIX_CHUNK = 8
OUT_TM = 512
OUT_SUB = 256


def _sigmoid(x):
    return 1.0 / (1.0 + jnp.exp2(x * (-LOG2E)))


def _silu(x):
    return x * _sigmoid(x)


def _rms(x, g):
    ms = jnp.mean(x * x, axis=-1, keepdims=True)
    return x * lax.rsqrt(ms + NORM_EPS) * g


def _layer_resident(x, l):
    nd = x.ndim - 1
    return pl.BlockSpec((None,) + x.shape[1:], lambda *_: (l,) + (0,) * nd, pipeline_mode=pl.Buffered(1))


def _inproj_kernel(h_ref, g_ref, w_ref, z_ref):
    u = _rms(h_ref[...], g_ref[...]).astype(BF16)
    for c in range(IN_WIDTH // INPROJ_TN):
        cols = slice(c * INPROJ_TN, (c + 1) * INPROJ_TN)
        z_ref[:, cols] = jnp.dot(u, w_ref[:, cols], preferred_element_type=F32).astype(z_ref.dtype)


def _inproj(h2d, g, w_b, l):
    m = h2d.shape[0]
    tm = INPROJ_TM
    return pl.pallas_call(
        _inproj_kernel,
        grid=(m // tm,),
        in_specs=[
            pl.BlockSpec((tm, D_MODEL), lambda i: (i, 0)),
            _layer_resident(g, l),
            _layer_resident(w_b, l),
        ],
        out_specs=pl.BlockSpec((tm, IN_WIDTH), lambda i: (i, 0)),
        out_shape=jax.ShapeDtypeStruct((m, IN_WIDTH), BF16),
        compiler_params=pltpu.CompilerParams(
            dimension_semantics=("arbitrary",), vmem_limit_bytes=VMEM_LIMIT),
        name="inproj",
    )(h2d, g, w_b)


def _bucket_table():
    q = np.arange(BLOCK)[:, None]
    c = np.arange(3 * BLOCK)[None, :]
    rel = c - BLOCK - q
    n = np.abs(rel)
    half = N_BUCKETS // 2
    max_exact = half // 2
    n2 = np.maximum(n, 1).astype(np.int64) ** 2
    floor_log2 = np.floor(np.log2(n2.astype(np.float64)) + 1e-9).astype(np.int64)
    large = np.minimum(max_exact + floor_log2 - 6, half - 1)
    bucket = np.where(rel > 0, half, 0) + np.where(n < max_exact, n, large)
    bucket = np.where(n <= WINDOW, bucket, -1)
    return bucket.astype(np.int32)


BIAS_INTERIOR, BIAS_FIRST, BIAS_LAST = 0, 1, 2


def _bias_kernel(tab_ref, bkt_ref, o_ref):
    h = pl.program_id(0)
    b = bkt_ref[...]
    acc = jnp.full(b.shape, NEG_INF, F32)
    for k in range(N_BUCKETS):
        acc = jnp.where(b == k, tab_ref[k, h] * LOG2E, acc)
    col = lax.broadcasted_iota(jnp.int32, b.shape, 1)
    o_ref[BIAS_INTERIOR, 0] = acc
    o_ref[BIAS_FIRST, 0] = jnp.where(col < BLOCK, NEG_INF, acc)
    o_ref[BIAS_LAST, 0] = jnp.where(col >= 2 * BLOCK, NEG_INF, acc)


def _rel_bias(rel_table):
    bkt = jnp.asarray(_bucket_table())
    return pl.pallas_call(
        _bias_kernel,
        grid=(N_HEADS,),
        in_specs=[
            pl.BlockSpec(memory_space=pltpu.SMEM),
            pl.BlockSpec((BLOCK, 3 * BLOCK), lambda h: (0, 0)),
        ],
        out_specs=pl.BlockSpec((3, 1, BLOCK, 3 * BLOCK), lambda h: (0, h, 0, 0)),
        out_shape=jax.ShapeDtypeStruct((3, N_HEADS, BLOCK, 3 * BLOCK), F32),
        name="relbias",
    )(rel_table, bkt)


def _attn_kernel(sink_ref, q_ref, kvp_ref, kvc_ref, kvn_ref, bias_ref, o_ref, *, nblk, layer):
    i = pl.program_id(1)
    ones = jnp.ones((3 * BLOCK, HEAD_DIM), BF16)
    windows = []
    for kh in range(N_KV_HEADS):
        ks = slice(kh * HEAD_DIM, (kh + 1) * HEAD_DIM)
        vs = slice(KV_WIDTH + kh * HEAD_DIM, KV_WIDTH + (kh + 1) * HEAD_DIM)
        windows.append((jnp.concatenate([kvp_ref[:, ks], kvc_ref[:, ks], kvn_ref[:, ks]], axis=0),
                        jnp.concatenate([kvp_ref[:, vs], kvc_ref[:, vs], kvn_ref[:, vs]], axis=0)))
    for kh, (k_win, v_win) in enumerate(windows):
        for s in range(ATTN_QB):
            n = i * ATTN_QB + s
            variant = jnp.where(n == 0, BIAS_FIRST, jnp.where(n == nblk - 1, BIAS_LAST, BIAS_INTERIOR))
            bias4 = bias_ref[variant, kh * Q_GROUP:(kh + 1) * Q_GROUP].reshape(Q_GROUP * BLOCK, 3 * BLOCK)
            rows = slice(s * BLOCK, (s + 1) * BLOCK)
            q4 = jnp.concatenate(
                [q_ref[rows, (kh * Q_GROUP + g) * HEAD_DIM:(kh * Q_GROUP + g + 1) * HEAD_DIM]
                 for g in range(Q_GROUP)], axis=0)
            k_sub = k_win[s * BLOCK:s * BLOCK + 3 * BLOCK]
            v_ext = jnp.concatenate([v_win[s * BLOCK:s * BLOCK + 3 * BLOCK], ones], axis=1)
            sc = lax.dot_general(q4, k_sub, (((1,), (1,)), ((), ())), preferred_element_type=F32) + bias4
            es, sink_terms = [], []
            for g in range(Q_GROUP):
                sg = sc[g * BLOCK:(g + 1) * BLOCK]
                sink2 = sink_ref[layer, kh * Q_GROUP + g] * LOG2E
                m = jnp.maximum(jnp.max(sg, axis=-1, keepdims=True), sink2)
                es.append(jnp.exp2(sg - m).astype(BF16))
                sink_terms.append(jnp.exp2(sink2 - m))
            pv = jnp.dot(jnp.concatenate(es, axis=0), v_ext, preferred_element_type=F32)
            for g in range(Q_GROUP):
                hd = kh * Q_GROUP + g
                pg = pv[g * BLOCK:(g + 1) * BLOCK]
                o = pg[:, :HEAD_DIM] / (pg[:, HEAD_DIM:] + sink_terms[g])
                o_ref[rows, hd * HEAD_DIM:(hd + 1) * HEAD_DIM] = o.astype(o_ref.dtype)


def _attention(z3, bias, sink, layer):
    b, s, _ = z3.shape
    nblk = s // BLOCK
    assert nblk >= 2 and nblk % ATTN_QB == 0
    tq = ATTN_QB * BLOCK
    kern = functools.partial(_attn_kernel, nblk=nblk, layer=layer)
    return pl.pallas_call(
        kern,
        grid=(b, s // tq),
        in_specs=[
            pl.BlockSpec(memory_space=pltpu.SMEM),
            pl.BlockSpec((None, tq, ATTN_WIDTH), lambda bi, i: (bi, i, 0)),
            pl.BlockSpec((None, BLOCK, COL), lambda bi, i: (bi, jnp.maximum(i * ATTN_QB - 1, 0), COL_KV)),
            pl.BlockSpec((None, tq, COL), lambda bi, i: (bi, i, COL_KV)),
            pl.BlockSpec((None, BLOCK, COL),
                         lambda bi, i: (bi, jnp.minimum((i + 1) * ATTN_QB, nblk - 1), COL_KV)),
            pl.BlockSpec(bias.shape, lambda bi, i: (0, 0, 0, 0)),
        ],
        out_specs=pl.BlockSpec((None, tq, ATTN_WIDTH), lambda bi, i: (bi, i, 0)),
        out_shape=jax.ShapeDtypeStruct((b, s, ATTN_WIDTH), BF16),
        compiler_params=pltpu.CompilerParams(
            dimension_semantics=("arbitrary", "arbitrary"), vmem_limit_bytes=VMEM_LIMIT),
        name="attn",
    )(sink, z3, z3, z3, z3, bias)


def _segment_permutation(t):
    groups = t // SUBLANES
    i = np.arange(t)
    p = np.zeros((t, t), np.float32)
    p[i, (i % SUBLANES) * groups + i // SUBLANES] = 1.0
    return p


def _lru_gates(xc, log_a_scale, wai_ref, ba_ref, bi_ref, d):
    ra, ri = [], []
    for hd in range(LRU_HEADS):
        xh = xc[:, hd * LRU_BLOCK:(hd + 1) * LRU_BLOCK].astype(BF16)
        y = jnp.dot(xh, wai_ref[d, hd], preferred_element_type=F32)
        ra.append(y[:, :LRU_BLOCK])
        ri.append(y[:, LRU_BLOCK:])
    r = _sigmoid(jnp.concatenate(ra, axis=1) + ba_ref[d:d + 1, :])
    gi = _sigmoid(jnp.concatenate(ri, axis=1) + bi_ref[d:d + 1, :])
    log_a = log_a_scale * r
    a = jnp.exp(log_a)
    v = -jnp.tanh(log_a) * (a * a + 1.0)
    root = jnp.where(v > 0.0, v * lax.rsqrt(v), 0.0)
    return a, root * (gi * xc)


def _lru_direction(x_ref, perm_ref, halo, carry, a_scr, u_scr, out_ref,
                   lw_ref, lb_ref, wai_ref, ba_ref, bi_ref, lam_ref, d):
    reverse = d == 1
    t = MIX_T
    groups = t // SUBLANES
    w = LRU_WIDTH
    k = LRU_CONV_K - 1
    nl = -lam_ref[d:d + 1, :]
    log_a_scale = -LRU_C * (jnp.maximum(nl, 0.0) + jnp.log1p(jnp.exp(-jnp.abs(nl))))
    x3 = jnp.dot(perm_ref[...], x_ref[...], preferred_element_type=F32).reshape(groups, SUBLANES, w)
    row = lax.broadcasted_iota(jnp.int32, (1, SUBLANES, w), 1)
    if reverse:
        src = x3[:k]
        wrapped = pltpu.roll(jnp.where(row == 0, halo[...], src), SUBLANES - 1, axis=1)
        ext = jnp.concatenate([x3, wrapped], axis=0)
    else:
        src = x3[groups - k:]
        wrapped = pltpu.roll(jnp.where(row == SUBLANES - 1, halo[...], src), 1, axis=1)
        ext = jnp.concatenate([wrapped, x3], axis=0)
    halo[...] = src
    taps = [lw_ref[d, j:j + 1, :] for j in range(LRU_CONV_K)]
    bias = lb_ref[d:d + 1, :]

    def in_time_order(n):
        return range(n - 1, -1, -1) if reverse else range(n)

    h = p = None
    for ci in in_time_order(groups // MIX_CHUNK):
        g0 = ci * MIX_CHUNK
        xc = bias + sum(taps[j] * ext[g0 + j:g0 + j + MIX_CHUNK] for j in range(LRU_CONV_K))
        a, u = _lru_gates(xc.reshape(MIX_CHUNK * SUBLANES, w), log_a_scale, wai_ref, ba_ref, bi_ref, d)
        a3 = a.reshape(MIX_CHUNK, SUBLANES, w)
        u3 = u.reshape(MIX_CHUNK, SUBLANES, w)
        a_scr[g0:g0 + MIX_CHUNK] = a3
        u_scr[g0:g0 + MIX_CHUNK] = u3
        for g in in_time_order(MIX_CHUNK):
            h = u3[g] if h is None else a3[g] * h + u3[g]
            p = a3[g] if p is None else a3[g] * p

    c = carry[0:1, :]
    cs = [None] * SUBLANES
    for r in in_time_order(SUBLANES):
        cs[r] = c
        c = p[r:r + 1, :] * c + h[r:r + 1, :]
    carry[0:1, :] = c

    h = jnp.concatenate(cs, axis=0)
    for g in in_time_order(groups):
        h = a_scr[g] * h + u_scr[g]
        for j in range(w // LANES):
            out_ref[j, pl.ds(g, SUBLANES, stride=groups), :] = h[:, j * LANES:(j + 1) * LANES]


def _mixers_kernel(xf_ref, xb_ref, perm_ref, lw_ref, lb_ref, wai_ref, ba_ref, bi_ref, lam_ref,
                   hf_ref, hb_ref, halo_f, halo_b, carry_f, carry_b, a_f, u_f, a_b, u_b):
    @pl.when(pl.program_id(1) == 0)
    def _():
        halo_f[...] = jnp.zeros(halo_f.shape, F32)
        halo_b[...] = jnp.zeros(halo_b.shape, F32)
        carry_f[...] = jnp.zeros(carry_f.shape, F32)
        carry_b[...] = jnp.zeros(carry_b.shape, F32)

    lru = (lw_ref, lb_ref, wai_ref, ba_ref, bi_ref, lam_ref)
    _lru_direction(xf_ref, perm_ref, halo_f, carry_f, a_f, u_f, hf_ref, *lru, 0)
    _lru_direction(xb_ref, perm_ref, halo_b, carry_b, a_b, u_b, hb_ref, *lru, 1)


def _mixers(z3, lru_conv_w, lru_conv_b, wai, b_a, b_i, lam, l):
    b, s, _ = z3.shape
    t = MIX_T
    nt = s // t
    slabs = LRU_WIDTH // LANES
    perm = jnp.asarray(_segment_permutation(t), BF16)
    params = (lru_conv_w, lru_conv_b, wai, b_a, b_i, lam)
    return pl.pallas_call(
        _mixers_kernel,
        grid=(b, nt),
        in_specs=[
            pl.BlockSpec((None, t, COL), lambda bi, i: (bi, i, COL_LRU_X)),
            pl.BlockSpec((None, t, COL), lambda bi, i: (bi, nt - 1 - i, COL_LRU_X)),
            pl.BlockSpec((t, t), lambda bi, i: (0, 0)),
        ] + [_layer_resident(x, l) for x in params],
        out_specs=[
            pl.BlockSpec((slabs, None, t, LANES), lambda bi, i: (0, bi, i, 0)),
            pl.BlockSpec((slabs, None, t, LANES), lambda bi, i: (0, bi, nt - 1 - i, 0)),
        ],
        out_shape=[
            jax.ShapeDtypeStruct((slabs, b, s, LANES), F32),
            jax.ShapeDtypeStruct((slabs, b, s, LANES), F32),
        ],
        scratch_shapes=[
            pltpu.VMEM((LRU_CONV_K - 1, SUBLANES, LRU_WIDTH), F32),
            pltpu.VMEM((LRU_CONV_K - 1, SUBLANES, LRU_WIDTH), F32),
            pltpu.VMEM((SUBLANES, LRU_WIDTH), F32),
            pltpu.VMEM((SUBLANES, LRU_WIDTH), F32),
        ] + [pltpu.VMEM((t // SUBLANES, SUBLANES, LRU_WIDTH), F32)] * 4,
        compiler_params=pltpu.CompilerParams(
            dimension_semantics=("arbitrary", "arbitrary"), vmem_limit_bytes=VMEM_LIMIT),
        name="mixers",
    )(z3, z3, perm, *params)


def _outple_kernel(h_ref, ya_ref, hf_ref, hb_ref, ga0_ref, ga1_ref, cb_ref, cc_ref, cx_ref, gc_ref, gl_ref,
                   ccp_ref, cxp_ref, ccn_ref, cxn_ref, p_ref,
                   cw_ref, wo_ref, pn_ref, wg_ref, wp_ref, fn_ref, o_ref, bufc, *, nt, last):
    i = pl.program_id(0)
    pad = SUBLANES
    tm = OUT_TM
    at_start = i % nt == 0
    at_end = i % nt == nt - 1
    before = (ccp_ref[...].astype(F32) * cxp_ref[...].astype(F32))[pad:]
    after = (ccn_ref[...].astype(F32) * cxn_ref[...].astype(F32))[:pad]
    bufc[0:pad, :] = jnp.where(at_start, 0.0, before)
    bufc[pad + tm:2 * pad + tm, :] = jnp.where(at_end, 0.0, after)
    chunk = MIX_CHUNK * SUBLANES
    for r0 in range(0, tm, chunk):
        rows = slice(r0, r0 + chunk)
        bufc[pad + r0:pad + r0 + chunk, :] = cc_ref[rows, :].astype(F32) * cx_ref[rows, :].astype(F32)

    subs = [slice(s * OUT_SUB, (s + 1) * OUT_SUB) for s in range(OUT_TM // OUT_SUB)]
    h1s, pps = [], []
    for rows in subs:
        g_attn = jnp.concatenate([ga0_ref[rows, :], ga1_ref[rows, :]], axis=1).astype(F32)
        y_attn = (ya_ref[rows, :].astype(F32) * _silu(g_attn)).astype(BF16)
        hsum = hf_ref[:, rows, :] + hb_ref[:, rows, :]
        y_lru = jnp.concatenate([hsum[j] for j in range(LRU_WIDTH // LANES)], axis=1)
        y_lru = (y_lru * _silu(gl_ref[rows, :].astype(F32))).astype(BF16)
        conv = sum(cw_ref[j:j + 1, :] * bufc[pad - 1 + j + rows.start:pad - 1 + j + rows.stop, :]
                   for j in range(CONV_K))
        y_conv = (cb_ref[rows, :].astype(F32) * conv * _silu(gc_ref[rows, :].astype(F32))).astype(BF16)
        proj = (jnp.dot(y_attn, wo_ref[0:ATTN_WIDTH, :], preferred_element_type=F32)
                + jnp.dot(y_lru, wo_ref[ATTN_WIDTH + CONV_WIDTH:, :], preferred_element_type=F32)
                + jnp.dot(y_conv, wo_ref[ATTN_WIDTH:ATTN_WIDTH + CONV_WIDTH, :], preferred_element_type=F32))
        h1s.append(h_ref[rows, :] + proj)
    for rows in subs:
        pps.append(jnp.dot(p_ref[rows, :].astype(BF16), wp_ref[...], preferred_element_type=F32))
    for rows, h1, pp in zip(subs, h1s, pps):
        n = _rms(h1, pn_ref[...]).astype(BF16)
        gate = _sigmoid(jnp.dot(n, wg_ref[...], preferred_element_type=F32))
        h2 = h1 + gate * pp
        if last:
            h2 = _rms(h2, fn_ref[...])
        o_ref[rows, :] = h2


def _outple(h2d, ya, hf, hb, z2d, p3d, cw, wo_b, pn, wg_b, wp_b, fn, l, nt, last):
    m = h2d.shape[0]
    tm = OUT_TM
    halo = 16
    kern = functools.partial(_outple_kernel, nt=nt, last=last)

    def rows(width, c=0):
        return pl.BlockSpec((tm, width), lambda i: (i, c))

    def halo_before(c):
        return pl.BlockSpec((halo, COL), lambda i: (jnp.maximum(i * (tm // halo) - 1, 0), c))

    def halo_after(c):
        return pl.BlockSpec((halo, COL), lambda i: (jnp.minimum((i + 1) * (tm // halo), m // halo - 1), c))

    slab = pl.BlockSpec((LRU_WIDTH // LANES, tm, LANES), lambda i: (0, i, 0))
    return pl.pallas_call(
        kern,
        grid=(m // tm,),
        in_specs=[
            rows(D_MODEL), rows(ATTN_WIDTH), slab, slab,
            rows(COL, COL_G_ATTN), rows(COL, COL_G_ATTN + 1), rows(COL, COL_CONV_B), rows(COL, COL_CONV_C),
            rows(COL, COL_CONV_X), rows(COL, COL_G_CONV), rows(COL, COL_G_LRU),
            halo_before(COL_CONV_C), halo_before(COL_CONV_X), halo_after(COL_CONV_C), halo_after(COL_CONV_X),
            pl.BlockSpec((None, tm, PLE_DIM), lambda i: (l, i, 0)),
            _layer_resident(cw, l), _layer_resident(wo_b, l), _layer_resident(pn, l), _layer_resident(wg_b, l),
            _layer_resident(wp_b, l), _layer_resident(fn, 0),
        ],
        out_specs=rows(D_MODEL),
        out_shape=jax.ShapeDtypeStruct((m, D_MODEL), F32),
        scratch_shapes=[pltpu.VMEM((tm + 2 * SUBLANES, CONV_WIDTH), F32)],
        compiler_params=pltpu.CompilerParams(
            dimension_semantics=("arbitrary",), vmem_limit_bytes=VMEM_LIMIT),
        name="outple",
    )(h2d, ya, hf, hb, *([z2d] * 11), p3d, cw, wo_b, pn, wg_b, wp_b, fn)


def _trunk(x, p, prm, bias):
    b, s, _ = x.shape
    m = b * s
    depth = p.shape[0]
    h = x.reshape(m, D_MODEL)
    p3d = p.reshape(depth, m, PLE_DIM)
    for l in range(depth):
        z = _inproj(h, prm["norm_mix"], prm["w_in"], l)
        z3 = z.reshape(b, s, IN_WIDTH)
        ya = _attention(z3, bias, prm["attn_sink"], l)
        hf, hb = _mixers(z3, prm["lru_conv_w"], prm["lru_conv_b"],
                         prm["wai"], prm["lru_b_a"], prm["lru_b_i"], prm["lru_L"], l)
        h = _outple(h, ya.reshape(m, ATTN_WIDTH), hf.reshape(-1, m, LANES), hb.reshape(-1, m, LANES), z, p3d,
                    prm["conv_w"], prm["w_out"], prm["ple_norm"], prm["ple_w_gate"], prm["ple_w_proj"],
                    prm["final_norm"], l, s // OUT_TM, last=(l == depth - 1))
    return h.reshape(b, s, D_MODEL)


def kernel(x_prompt, x_sample, p_prompt, p_sample, norm_mix, w_in, w_out, rel_bias, attn_sink, conv_w,
           lru_conv_w, lru_conv_b, lru_w_a, lru_b_a, lru_w_i, lru_b_i, lru_L, ple_norm, ple_w_gate,
           ple_w_proj, final_norm):
    col_scale = jnp.where(jnp.arange(IN_WIDTH) < ATTN_WIDTH, Q_SCALE, 1.0).astype(F32)
    prm = dict(
        norm_mix=norm_mix[:, None, :], w_in=(w_in * col_scale).astype(BF16), w_out=w_out.astype(BF16),
        attn_sink=attn_sink, conv_w=conv_w, lru_conv_w=lru_conv_w, lru_conv_b=lru_conv_b,
        wai=jnp.concatenate([lru_w_a, lru_w_i], axis=-1).astype(BF16),
        lru_b_a=lru_b_a, lru_b_i=lru_b_i, lru_L=lru_L, ple_norm=ple_norm[:, None, :],
        ple_w_gate=ple_w_gate.astype(BF16), ple_w_proj=ple_w_proj.astype(BF16),
        final_norm=final_norm[None, None, :])
    bias = _rel_bias(rel_bias)
    y_prompt = _trunk(x_prompt, p_prompt, prm, bias)
    y_sample = _trunk(x_sample, p_sample, prm, bias)
    return (y_prompt, y_sample)
```

```python
import functools
import math

import numpy as np
import jax
import jax.numpy as jnp
from jax import lax
from jax.experimental import pallas as pl
from jax.experimental.pallas import tpu as pltpu

F32 = jnp.float32
BF16 = jnp.bfloat16

D_MODEL = 2048
N_HEADS = 8
N_KV_HEADS = 2
Q_GROUP = N_HEADS // N_KV_HEADS
HEAD_DIM = 128
ATTN_WIDTH = N_HEADS * HEAD_DIM
KV_WIDTH = N_KV_HEADS * HEAD_DIM
WINDOW = 128
BLOCK = 128
N_BUCKETS = 32
MAX_DISTANCE = 128
CONV_WIDTH = 512
CONV_K = 3
LRU_WIDTH = 512
LRU_HEADS = 4
LRU_BLOCK = 128
LRU_CONV_K = 4
LRU_C = 8.0
MIX_WIDTH = ATTN_WIDTH + CONV_WIDTH + LRU_WIDTH
IN_WIDTH = 2 * ATTN_WIDTH + 2 * KV_WIDTH + 4 * CONV_WIDTH + 2 * LRU_WIDTH
PLE_DIM = 256
NORM_EPS = 1e-6
NEG_INF = -1e30
LOG2E = math.log2(math.e)
Q_SCALE = HEAD_DIM ** -0.5 * LOG2E

COL = 512
COL_KV = 2
COL_G_ATTN = 3
COL_CONV_B = 5
COL_CONV_C = 6
COL_CONV_X = 7
COL_G_CONV = 8
COL_LRU_X = 9
COL_G_LRU = 10

SUBLANES = 8
LANES = 128
VMEM_LIMIT = 60 * 1024 * 1024

INPROJ_TM = 512
INPROJ_TN = 512
ATTN_QB = 4
MIX_T = 512
MIX_CHUNK = 8
OUT_TM = 512
OUT_SUB = 256


def _sigmoid(x):
    return 1.0 / (1.0 + jnp.exp2(x * (-LOG2E)))


def _silu(x):
    return x * _sigmoid(x)


def _rms(x, g):
    ms = jnp.mean(x * x, axis=-1, keepdims=True)
    return x * lax.rsqrt(ms + NORM_EPS) * g


def _layer_resident(x, l):
    nd = x.ndim - 1
    return pl.BlockSpec((None,) + x.shape[1:], lambda *_: (l,) + (0,) * nd, pipeline_mode=pl.Buffered(1))


def _inproj_kernel(h_ref, g_ref, w_ref, z_ref):
    u = _rms(h_ref[...], g_ref[...]).astype(BF16)
    for c in range(IN_WIDTH // INPROJ_TN):
        cols = slice(c * INPROJ_TN, (c + 1) * INPROJ_TN)
        z_ref[:, cols] = jnp.dot(u, w_ref[:, cols], preferred_element_type=F32).astype(z_ref.dtype)


def _inproj(h2d, g, w_b, l):
    m = h2d.shape[0]
    tm = INPROJ_TM
    return pl.pallas_call(
        _inproj_kernel,
        grid=(m // tm,),
        in_specs=[
            pl.BlockSpec((tm, D_MODEL), lambda i: (i, 0)),
            _layer_resident(g, l),
            _layer_resident(w_b, l),
        ],
        out_specs=pl.BlockSpec((tm, IN_WIDTH), lambda i: (i, 0)),
        out_shape=jax.ShapeDtypeStruct((m, IN_WIDTH), BF16),
        compiler_params=pltpu.CompilerParams(
            dimension_semantics=("arbitrary",), vmem_limit_bytes=VMEM_LIMIT),
        name="inproj",
    )(h2d, g, w_b)


def _bucket_table():
    q = np.arange(BLOCK)[:, None]
    c = np.arange(3 * BLOCK)[None, :]
    rel = c - BLOCK - q
    n = np.abs(rel)
    half = N_BUCKETS // 2
    max_exact = half // 2
    n2 = np.maximum(n, 1).astype(np.int64) ** 2
    floor_log2 = np.floor(np.log2(n2.astype(np.float64)) + 1e-9).astype(np.int64)
    large = np.minimum(max_exact + floor_log2 - 6, half - 1)
    bucket = np.where(rel > 0, half, 0) + np.where(n < max_exact, n, large)
    bucket = np.where(n <= WINDOW, bucket, -1)
    return bucket.astype(np.int32)


BIAS_INTERIOR, BIAS_FIRST, BIAS_LAST = 0, 1, 2


def _bias_kernel(tab_ref, bkt_ref, o_ref):
    h = pl.program_id(0)
    b = bkt_ref[...]
    acc = jnp.full(b.shape, NEG_INF, F32)
    for k in range(N_BUCKETS):
        acc = jnp.where(b == k, tab_ref[k, h] * LOG2E, acc)
    col = lax.broadcasted_iota(jnp.int32, b.shape, 1)
    o_ref[BIAS_INTERIOR, 0] = acc
    o_ref[BIAS_FIRST, 0] = jnp.where(col < BLOCK, NEG_INF, acc)
    o_ref[BIAS_LAST, 0] = jnp.where(col >= 2 * BLOCK, NEG_INF, acc)


def _rel_bias(rel_table):
    bkt = jnp.asarray(_bucket_table())
    return pl.pallas_call(
        _bias_kernel,
        grid=(N_HEADS,),
        in_specs=[
            pl.BlockSpec(memory_space=pltpu.SMEM),
            pl.BlockSpec((BLOCK, 3 * BLOCK), lambda h: (0, 0)),
        ],
        out_specs=pl.BlockSpec((3, 1, BLOCK, 3 * BLOCK), lambda h: (0, h, 0, 0)),
        out_shape=jax.ShapeDtypeStruct((3, N_HEADS, BLOCK, 3 * BLOCK), F32),
        name="relbias",
    )(rel_table, bkt)


def _attn_kernel(sink_ref, q_ref, kvp_ref, kvc_ref, kvn_ref, bias_ref, o_ref, *, nblk, layer):
    i = pl.program_id(1)
    ones = jnp.ones((3 * BLOCK, HEAD_DIM), BF16)
    windows = []
    for kh in range(N_KV_HEADS):
        ks = slice(kh * HEAD_DIM, (kh + 1) * HEAD_DIM)
        vs = slice(KV_WIDTH + kh * HEAD_DIM, KV_WIDTH + (kh + 1) * HEAD_DIM)
        windows.append((jnp.concatenate([kvp_ref[:, ks], kvc_ref[:, ks], kvn_ref[:, ks]], axis=0),
                        jnp.concatenate([kvp_ref[:, vs], kvc_ref[:, vs], kvn_ref[:, vs]], axis=0)))
    for kh, (k_win, v_win) in enumerate(windows):
        for s in range(ATTN_QB):
            n = i * ATTN_QB + s
            variant = jnp.where(n == 0, BIAS_FIRST, jnp.where(n == nblk - 1, BIAS_LAST, BIAS_INTERIOR))
            bias4 = bias_ref[variant, kh * Q_GROUP:(kh + 1) * Q_GROUP].reshape(Q_GROUP * BLOCK, 3 * BLOCK)
            rows = slice(s * BLOCK, (s + 1) * BLOCK)
            q4 = jnp.concatenate(
                [q_ref[rows, (kh * Q_GROUP + g) * HEAD_DIM:(kh * Q_GROUP + g + 1) * HEAD_DIM]
                 for g in range(Q_GROUP)], axis=0)
            k_sub = k_win[s * BLOCK:s * BLOCK + 3 * BLOCK]
            v_ext = jnp.concatenate([v_win[s * BLOCK:s * BLOCK + 3 * BLOCK], ones], axis=1)
            sc = lax.dot_general(q4, k_sub, (((1,), (1,)), ((), ())), preferred_element_type=F32) + bias4
            es, sink_terms = [], []
            for g in range(Q_GROUP):
                sg = sc[g * BLOCK:(g + 1) * BLOCK]
                sink2 = sink_ref[layer, kh * Q_GROUP + g] * LOG2E
                m = jnp.maximum(jnp.max(sg, axis=-1, keepdims=True), sink2)
                es.append(jnp.exp2(sg - m).astype(BF16))
                sink_terms.append(jnp.exp2(sink2 - m))
            pv = jnp.dot(jnp.concatenate(es, axis=0), v_ext, preferred_element_type=F32)
            for g in range(Q_GROUP):
                hd = kh * Q_GROUP + g
                pg = pv[g * BLOCK:(g + 1) * BLOCK]
                o = pg[:, :HEAD_DIM] / (pg[:, HEAD_DIM:] + sink_terms[g])
                o_ref[rows, hd * HEAD_DIM:(hd + 1) * HEAD_DIM] = o.astype(o_ref.dtype)


def _attention(z3, bias, sink, layer):
    b, s, _ = z3.shape
    nblk = s // BLOCK
    assert nblk >= 2 and nblk % ATTN_QB == 0
    tq = ATTN_QB * BLOCK
    kern = functools.partial(_attn_kernel, nblk=nblk, layer=layer)
    return pl.pallas_call(
        kern,
        grid=(b, s // tq),
        in_specs=[
            pl.BlockSpec(memory_space=pltpu.SMEM),
            pl.BlockSpec((None, tq, ATTN_WIDTH), lambda bi, i: (bi, i, 0)),
            pl.BlockSpec((None, BLOCK, COL), lambda bi, i: (bi, jnp.maximum(i * ATTN_QB - 1, 0), COL_KV)),
            pl.BlockSpec((None, tq, COL), lambda bi, i: (bi, i, COL_KV)),
            pl.BlockSpec((None, BLOCK, COL),
                         lambda bi, i: (bi, jnp.minimum((i + 1) * ATTN_QB, nblk - 1), COL_KV)),
            pl.BlockSpec(bias.shape, lambda bi, i: (0, 0, 0, 0)),
        ],
        out_specs=pl.BlockSpec((None, tq, ATTN_WIDTH), lambda bi, i: (bi, i, 0)),
        out_shape=jax.ShapeDtypeStruct((b, s, ATTN_WIDTH), BF16),
        compiler_params=pltpu.CompilerParams(
            dimension_semantics=("arbitrary", "arbitrary"), vmem_limit_bytes=VMEM_LIMIT),
        name="attn",
    )(sink, z3, z3, z3, z3, bias)


def _segment_permutation(t):
    groups = t // SUBLANES
    i = np.arange(t)
    p = np.zeros((t, t), np.float32)
    p[i, (i % SUBLANES) * groups + i // SUBLANES] = 1.0
    return p


def _lru_gates(xc, log_a_scale, wai_ref, ba_ref, bi_ref, d):
    ra, ri = [], []
    for hd in range(LRU_HEADS):
        xh = xc[:, hd * LRU_BLOCK:(hd + 1) * LRU_BLOCK].astype(BF16)
        y = jnp.dot(xh, wai_ref[d, hd], preferred_element_type=F32)
        ra.append(y[:, :LRU_BLOCK])
        ri.append(y[:, LRU_BLOCK:])
    r = _sigmoid(jnp.concatenate(ra, axis=1) + ba_ref[d:d + 1, :])
    gi = _sigmoid(jnp.concatenate(ri, axis=1) + bi_ref[d:d + 1, :])
    log_a = log_a_scale * r
    a = jnp.exp(log_a)
    v = -jnp.tanh(log_a) * (a * a + 1.0)
    root = jnp.where(v > 0.0, v * lax.rsqrt(v), 0.0)
    return a, root * (gi * xc)


def _lru_direction(x3_scr, halo, carry, a_scr, u_scr, out_ref,
                   lw_ref, lb_ref, wai_ref, ba_ref, bi_ref, lam_ref, d):
    reverse = d == 1
    t = MIX_T
    groups = t // SUBLANES
    w = LRU_WIDTH
    k = LRU_CONV_K - 1
    nl = -lam_ref[d:d + 1, :]
    log_a_scale = -LRU_C * (jnp.maximum(nl, 0.0) + jnp.log1p(jnp.exp(-jnp.abs(nl))))
    x3 = x3_scr[...]
    row = lax.broadcasted_iota(jnp.int32, (1, SUBLANES, w), 1)
    if reverse:
        src = x3[:k]
        wrapped = pltpu.roll(jnp.where(row == 0, halo[...], src), SUBLANES - 1, axis=1)
        ext = jnp.concatenate([x3, wrapped], axis=0)
    else:
        src = x3[groups - k:]
        wrapped = pltpu.roll(jnp.where(row == SUBLANES - 1, halo[...], src), 1, axis=1)
        ext = jnp.concatenate([wrapped, x3], axis=0)
    halo[...] = src
    taps = [lw_ref[d, j:j + 1, :] for j in range(LRU_CONV_K)]
    bias = lb_ref[d:d + 1, :]

    def in_time_order(n):
        return range(n - 1, -1, -1) if reverse else range(n)

    h = p = None
    for ci in in_time_order(groups // MIX_CHUNK):
        g0 = ci * MIX_CHUNK
        xc = bias + sum(taps[j] * ext[g0 + j:g0 + j + MIX_CHUNK] for j in range(LRU_CONV_K))
        a, u = _lru_gates(xc.reshape(MIX_CHUNK * SUBLANES, w), log_a_scale, wai_ref, ba_ref, bi_ref, d)
        a3 = a.reshape(MIX_CHUNK, SUBLANES, w)
        u3 = u.reshape(MIX_CHUNK, SUBLANES, w)
        a_scr[g0:g0 + MIX_CHUNK] = a3
        u_scr[g0:g0 + MIX_CHUNK] = u3
        for g in in_time_order(MIX_CHUNK):
            h = u3[g] if h is None else a3[g] * h + u3[g]
            p = a3[g] if p is None else a3[g] * p

    c = carry[0:1, :]
    cs = [None] * SUBLANES
    for r in in_time_order(SUBLANES):
        cs[r] = c
        c = p[r:r + 1, :] * c + h[r:r + 1, :]
    carry[0:1, :] = c

    h = jnp.concatenate(cs, axis=0)
    for g in in_time_order(groups):
        h = a_scr[g] * h + u_scr[g]
        for j in range(w // LANES):
            out_ref[j, pl.ds(g, SUBLANES, stride=groups), :] = h[:, j * LANES:(j + 1) * LANES]


def _mixers_kernel(xf_ref, xb_ref, xfn_ref, xbn_ref, perm_ref, lw_ref, lb_ref, wai_ref, ba_ref, bi_ref, lam_ref,
                   hf_ref, hb_ref, x3_f, x3_b, halo_f, halo_b, carry_f, carry_b, a_f, u_f, a_b, u_b):
    def permuted(x_ref):
        x3 = jnp.dot(perm_ref[...], x_ref[...], preferred_element_type=F32)
        return x3.reshape(MIX_T // SUBLANES, SUBLANES, LRU_WIDTH)

    @pl.when(pl.program_id(1) == 0)
    def _():
        halo_f[...] = jnp.zeros(halo_f.shape, F32)
        halo_b[...] = jnp.zeros(halo_b.shape, F32)
        carry_f[...] = jnp.zeros(carry_f.shape, F32)
        carry_b[...] = jnp.zeros(carry_b.shape, F32)

    @pl.when((pl.program_id(0) == 0) & (pl.program_id(1) == 0))
    def _():
        x3_f[...] = permuted(xf_ref)
        x3_b[...] = permuted(xb_ref)

    lru = (lw_ref, lb_ref, wai_ref, ba_ref, bi_ref, lam_ref)
    _lru_direction(x3_f, halo_f, carry_f, a_f, u_f, hf_ref, *lru, 0)
    _lru_direction(x3_b, halo_b, carry_b, a_b, u_b, hb_ref, *lru, 1)
    x3_f[...] = permuted(xfn_ref)
    x3_b[...] = permuted(xbn_ref)


def _mixers(z3, lru_conv_w, lru_conv_b, wai, b_a, b_i, lam, l):
    b, s, _ = z3.shape
    t = MIX_T
    nt = s // t
    slabs = LRU_WIDTH // LANES
    perm = jnp.asarray(_segment_permutation(t), BF16)
    params = (lru_conv_w, lru_conv_b, wai, b_a, b_i, lam)

    def next_tile(bi, i, reverse):
        step = jnp.minimum(bi * nt + i + 1, b * nt - 1)
        ni = step % nt
        return (step // nt, nt - 1 - ni if reverse else ni, COL_LRU_X)

    return pl.pallas_call(
        _mixers_kernel,
        grid=(b, nt),
        in_specs=[
            pl.BlockSpec((None, t, COL), lambda bi, i: (bi, i, COL_LRU_X)),
            pl.BlockSpec((None, t, COL), lambda bi, i: (bi, nt - 1 - i, COL_LRU_X)),
            pl.BlockSpec((None, t, COL), lambda bi, i: next_tile(bi, i, False)),
            pl.BlockSpec((None, t, COL), lambda bi, i: next_tile(bi, i, True)),
            pl.BlockSpec((t, t), lambda bi, i: (0, 0)),
        ] + [_layer_resident(x, l) for x in params],
        out_specs=[
            pl.BlockSpec((slabs, None, t, LANES), lambda bi, i: (0, bi, i, 0)),
            pl.BlockSpec((slabs, None, t, LANES), lambda bi, i: (0, bi, nt - 1 - i, 0)),
        ],
        out_shape=[
            jax.ShapeDtypeStruct((slabs, b, s, LANES), F32),
            jax.ShapeDtypeStruct((slabs, b, s, LANES), F32),
        ],
        scratch_shapes=[pltpu.VMEM((t // SUBLANES, SUBLANES, LRU_WIDTH), F32)] * 2 + [
            pltpu.VMEM((LRU_CONV_K - 1, SUBLANES, LRU_WIDTH), F32),
            pltpu.VMEM((LRU_CONV_K - 1, SUBLANES, LRU_WIDTH), F32),
            pltpu.VMEM((SUBLANES, LRU_WIDTH), F32),
            pltpu.VMEM((SUBLANES, LRU_WIDTH), F32),
        ] + [pltpu.VMEM((t // SUBLANES, SUBLANES, LRU_WIDTH), F32)] * 4,
        compiler_params=pltpu.CompilerParams(
            dimension_semantics=("arbitrary", "arbitrary"), vmem_limit_bytes=VMEM_LIMIT),
        name="mixers",
    )(z3, z3, z3, z3, perm, *params)


def _outple_kernel(h_ref, ya_ref, hf_ref, hb_ref, ga0_ref, ga1_ref, cb_ref, cc_ref, cx_ref, gc_ref, gl_ref,
                   ccp_ref, cxp_ref, ccn_ref, cxn_ref, p_ref,
                   cw_ref, wo_ref, pn_ref, wg_ref, wp_ref, fn_ref, o_ref, bufc, *, nt, last):
    i = pl.program_id(0)
    pad = SUBLANES
    tm = OUT_TM
    at_start = i % nt == 0
    at_end = i % nt == nt - 1
    before = (ccp_ref[...].astype(F32) * cxp_ref[...].astype(F32))[pad:]
    after = (ccn_ref[...].astype(F32) * cxn_ref[...].astype(F32))[:pad]
    bufc[0:pad, :] = jnp.where(at_start, 0.0, before)
    bufc[pad + tm:2 * pad + tm, :] = jnp.where(at_end, 0.0, after)
    chunk = MIX_CHUNK * SUBLANES
    for r0 in range(0, tm, chunk):
        rows = slice(r0, r0 + chunk)
        bufc[pad + r0:pad + r0 + chunk, :] = cc_ref[rows, :].astype(F32) * cx_ref[rows, :].astype(F32)

    subs = [slice(s * OUT_SUB, (s + 1) * OUT_SUB) for s in range(OUT_TM // OUT_SUB)]
    h1s, pps = [], []
    for rows in subs:
        g_attn = jnp.concatenate([ga0_ref[rows, :], ga1_ref[rows, :]], axis=1).astype(F32)
        y_attn = (ya_ref[rows, :].astype(F32) * _silu(g_attn)).astype(BF16)
        hsum = hf_ref[:, rows, :] + hb_ref[:, rows, :]
        y_lru = jnp.concatenate([hsum[j] for j in range(LRU_WIDTH // LANES)], axis=1)
        y_lru = (y_lru * _silu(gl_ref[rows, :].astype(F32))).astype(BF16)
        conv = sum(cw_ref[j:j + 1, :] * bufc[pad - 1 + j + rows.start:pad - 1 + j + rows.stop, :]
                   for j in range(CONV_K))
        y_conv = (cb_ref[rows, :].astype(F32) * conv * _silu(gc_ref[rows, :].astype(F32))).astype(BF16)
        proj = (jnp.dot(y_attn, wo_ref[0:ATTN_WIDTH, :], preferred_element_type=F32)
                + jnp.dot(y_lru, wo_ref[ATTN_WIDTH + CONV_WIDTH:, :], preferred_element_type=F32)
                + jnp.dot(y_conv, wo_ref[ATTN_WIDTH:ATTN_WIDTH + CONV_WIDTH, :], preferred_element_type=F32))
        h1s.append(h_ref[rows, :] + proj)
    for rows in subs:
        pps.append(jnp.dot(p_ref[rows, :].astype(BF16), wp_ref[...], preferred_element_type=F32))
    for rows, h1, pp in zip(subs, h1s, pps):
        n = _rms(h1, pn_ref[...]).astype(BF16)
        gate = _sigmoid(jnp.dot(n, wg_ref[...], preferred_element_type=F32))
        h2 = h1 + gate * pp
        if last:
            h2 = _rms(h2, fn_ref[...])
        o_ref[rows, :] = h2


def _outple(h2d, ya, hf, hb, z2d, p3d, cw, wo_b, pn, wg_b, wp_b, fn, l, nt, last):
    m = h2d.shape[0]
    tm = OUT_TM
    halo = 16
    kern = functools.partial(_outple_kernel, nt=nt, last=last)

    def rows(width, c=0):
        return pl.BlockSpec((tm, width), lambda i: (i, c))

    def halo_before(c):
        return pl.BlockSpec((halo, COL), lambda i: (jnp.maximum(i * (tm // halo) - 1, 0), c))

    def halo_after(c):
        return pl.BlockSpec((halo, COL), lambda i: (jnp.minimum((i + 1) * (tm // halo), m // halo - 1), c))

    slab = pl.BlockSpec((LRU_WIDTH // LANES, tm, LANES), lambda i: (0, i, 0))
    return pl.pallas_call(
        kern,
        grid=(m // tm,),
        in_specs=[
            rows(D_MODEL), rows(ATTN_WIDTH), slab, slab,
            rows(COL, COL_G_ATTN), rows(COL, COL_G_ATTN + 1), rows(COL, COL_CONV_B), rows(COL, COL_CONV_C),
            rows(COL, COL_CONV_X), rows(COL, COL_G_CONV), rows(COL, COL_G_LRU),
            halo_before(COL_CONV_C), halo_before(COL_CONV_X), halo_after(COL_CONV_C), halo_after(COL_CONV_X),
            pl.BlockSpec((None, tm, PLE_DIM), lambda i: (l, i, 0)),
            _layer_resident(cw, l), _layer_resident(wo_b, l), _layer_resident(pn, l), _layer_resident(wg_b, l),
            _layer_resident(wp_b, l), _layer_resident(fn, 0),
        ],
        out_specs=rows(D_MODEL),
        out_shape=jax.ShapeDtypeStruct((m, D_MODEL), F32),
        scratch_shapes=[pltpu.VMEM((tm + 2 * SUBLANES, CONV_WIDTH), F32)],
        compiler_params=pltpu.CompilerParams(
            dimension_semantics=("arbitrary",), vmem_limit_bytes=VMEM_LIMIT),
        name="outple",
    )(h2d, ya, hf, hb, *([z2d] * 11), p3d, cw, wo_b, pn, wg_b, wp_b, fn)


def _trunk(x, p, prm, bias):
    b, s, _ = x.shape
    m = b * s
    depth = p.shape[0]
    h = x.reshape(m, D_MODEL)
    p3d = p.reshape(depth, m, PLE_DIM)
    for l in range(depth):
        z = _inproj(h, prm["norm_mix"], prm["w_in"], l)
        z3 = z.reshape(b, s, IN_WIDTH)
        ya = _attention(z3, bias, prm["attn_sink"], l)
        hf, hb = _mixers(z3, prm["lru_conv_w"], prm["lru_conv_b"],
                         prm["wai"], prm["lru_b_a"], prm["lru_b_i"], prm["lru_L"], l)
        h = _outple(h, ya.reshape(m, ATTN_WIDTH), hf.reshape(-1, m, LANES), hb.reshape(-1, m, LANES), z, p3d,
                    prm["conv_w"], prm["w_out"], prm["ple_norm"], prm["ple_w_gate"], prm["ple_w_proj"],
                    prm["final_norm"], l, s // OUT_TM, last=(l == depth - 1))
    return h.reshape(b, s, D_MODEL)


def kernel(x_prompt, x_sample, p_prompt, p_sample, norm_mix, w_in, w_out, rel_bias, attn_sink, conv_w,
           lru_conv_w, lru_conv_b, lru_w_a, lru_b_a, lru_w_i, lru_b_i, lru_L, ple_norm, ple_w_gate,
           ple_w_proj, final_norm):
    col_scale = jnp.where(jnp.arange(IN_WIDTH) < ATTN_WIDTH, Q_SCALE, 1.0).astype(F32)
    prm = dict(
        norm_mix=norm_mix[:, None, :], w_in=(w_in * col_scale).astype(BF16), w_out=w_out.astype(BF16),
        attn_sink=attn_sink, conv_w=conv_w, lru_conv_w=lru_conv_w, lru_conv_b=lru_conv_b,
        wai=jnp.concatenate([lru_w_a, lru_w_i], axis=-1).astype(BF16),
        lru_b_a=lru_b_a, lru_b_i=lru_b_i, lru_L=lru_L, ple_norm=ple_norm[:, None, :],
        ple_w_gate=ple_w_gate.astype(BF16), ple_w_proj=ple_w_proj.astype(BF16),
        final_norm=final_norm[None, None, :])
    bias = _rel_bias(rel_bias)
    y_prompt = _trunk(x_prompt, p_prompt, prm, bias)
    y_sample = _trunk(x_sample, p_sample, prm, bias)
    return (y_prompt, y_sample)
```

```python
import functools
import math

import numpy as np
import jax
import jax.numpy as jnp
from jax import lax
from jax.experimental import pallas as pl
from jax.experimental.pallas import tpu as pltpu

F32 = jnp.float32
BF16 = jnp.bfloat16

D_MODEL = 2048
N_HEADS = 8
N_KV_HEADS = 2
Q_GROUP = N_HEADS // N_KV_HEADS
HEAD_DIM = 128
ATTN_WIDTH = N_HEADS * HEAD_DIM
KV_WIDTH = N_KV_HEADS * HEAD_DIM
WINDOW = 128
BLOCK = 128
N_BUCKETS = 32
MAX_DISTANCE = 128
CONV_WIDTH = 512
CONV_K = 3
LRU_WIDTH = 512
LRU_HEADS = 4
LRU_BLOCK = 128
LRU_CONV_K = 4
LRU_C = 8.0
MIX_WIDTH = ATTN_WIDTH + CONV_WIDTH + LRU_WIDTH
IN_WIDTH = 2 * ATTN_WIDTH + 2 * KV_WIDTH + 4 * CONV_WIDTH + 2 * LRU_WIDTH
PLE_DIM = 256
NORM_EPS = 1e-6
NEG_INF = -1e30
LOG2E = math.log2(math.e)
Q_SCALE = HEAD_DIM ** -0.5 * LOG2E

COL = 512
COL_KV = 2
COL_G_ATTN = 3
COL_CONV_B = 5
COL_CONV_C = 6
COL_CONV_X = 7
COL_G_CONV = 8
COL_LRU_X = 9
COL_G_LRU = 10

SUBLANES = 8
LANES = 128
VMEM_LIMIT = 60 * 1024 * 1024

INPROJ_TM = 512
INPROJ_TN = 512
ATTN_QB = 4
MIX_T = 512
MIX_CHUNK = 8
SEG_ROWS = MIX_T // SUBLANES
SEG_PITCH = SEG_ROWS + SUBLANES
TILE_PITCH = SUBLANES * SEG_PITCH
OUT_TM = 512
OUT_SUB = 256


def _sigmoid(x):
    return 1.0 / (1.0 + jnp.exp2(x * (-LOG2E)))


def _silu(x):
    return x * _sigmoid(x)


def _rms(x, g):
    ms = jnp.mean(x * x, axis=-1, keepdims=True)
    return x * lax.rsqrt(ms + NORM_EPS) * g


def _layer_resident(x, l):
    nd = x.ndim - 1
    return pl.BlockSpec((None,) + x.shape[1:], lambda *_: (l,) + (0,) * nd, pipeline_mode=pl.Buffered(1))


def _inproj_kernel(h_ref, g_ref, w_ref, z_ref):
    u = _rms(h_ref[...], g_ref[...]).astype(BF16)
    for c in range(IN_WIDTH // INPROJ_TN):
        cols = slice(c * INPROJ_TN, (c + 1) * INPROJ_TN)
        z_ref[:, cols] = jnp.dot(u, w_ref[:, cols], preferred_element_type=F32).astype(z_ref.dtype)


def _inproj(h2d, g, w_b, l):
    m = h2d.shape[0]
    tm = INPROJ_TM
    return pl.pallas_call(
        _inproj_kernel,
        grid=(m // tm,),
        in_specs=[
            pl.BlockSpec((tm, D_MODEL), lambda i: (i, 0)),
            _layer_resident(g, l),
            _layer_resident(w_b, l),
        ],
        out_specs=pl.BlockSpec((tm, IN_WIDTH), lambda i: (i, 0)),
        out_shape=jax.ShapeDtypeStruct((m, IN_WIDTH), BF16),
        compiler_params=pltpu.CompilerParams(
            dimension_semantics=("arbitrary",), vmem_limit_bytes=VMEM_LIMIT),
        name="inproj",
    )(h2d, g, w_b)


def _bucket_table():
    q = np.arange(BLOCK)[:, None]
    c = np.arange(3 * BLOCK)[None, :]
    rel = c - BLOCK - q
    n = np.abs(rel)
    half = N_BUCKETS // 2
    max_exact = half // 2
    n2 = np.maximum(n, 1).astype(np.int64) ** 2
    floor_log2 = np.floor(np.log2(n2.astype(np.float64)) + 1e-9).astype(np.int64)
    large = np.minimum(max_exact + floor_log2 - 6, half - 1)
    bucket = np.where(rel > 0, half, 0) + np.where(n < max_exact, n, large)
    bucket = np.where(n <= WINDOW, bucket, -1)
    return bucket.astype(np.int32)


BIAS_INTERIOR, BIAS_FIRST, BIAS_LAST = 0, 1, 2


def _bias_kernel(tab_ref, bkt_ref, o_ref):
    h = pl.program_id(0)
    b = bkt_ref[...]
    acc = jnp.full(b.shape, NEG_INF, F32)
    for k in range(N_BUCKETS):
        acc = jnp.where(b == k, tab_ref[k, h] * LOG2E, acc)
    col = lax.broadcasted_iota(jnp.int32, b.shape, 1)
    o_ref[BIAS_INTERIOR, 0] = acc
    o_ref[BIAS_FIRST, 0] = jnp.where(col < BLOCK, NEG_INF, acc)
    o_ref[BIAS_LAST, 0] = jnp.where(col >= 2 * BLOCK, NEG_INF, acc)


def _rel_bias(rel_table):
    bkt = jnp.asarray(_bucket_table())
    return pl.pallas_call(
        _bias_kernel,
        grid=(N_HEADS,),
        in_specs=[
            pl.BlockSpec(memory_space=pltpu.SMEM),
            pl.BlockSpec((BLOCK, 3 * BLOCK), lambda h: (0, 0)),
        ],
        out_specs=pl.BlockSpec((3, 1, BLOCK, 3 * BLOCK), lambda h: (0, h, 0, 0)),
        out_shape=jax.ShapeDtypeStruct((3, N_HEADS, BLOCK, 3 * BLOCK), F32),
        name="relbias",
    )(rel_table, bkt)


def _attn_kernel(sink_ref, q_ref, kvp_ref, kvc_ref, kvn_ref, bias_ref, o_ref, *, nblk, layer):
    i = pl.program_id(1)
    ones = jnp.ones((3 * BLOCK, HEAD_DIM), BF16)
    windows = []
    for kh in range(N_KV_HEADS):
        ks = slice(kh * HEAD_DIM, (kh + 1) * HEAD_DIM)
        vs = slice(KV_WIDTH + kh * HEAD_DIM, KV_WIDTH + (kh + 1) * HEAD_DIM)
        windows.append((jnp.concatenate([kvp_ref[:, ks], kvc_ref[:, ks], kvn_ref[:, ks]], axis=0),
                        jnp.concatenate([kvp_ref[:, vs], kvc_ref[:, vs], kvn_ref[:, vs]], axis=0)))
    for kh, (k_win, v_win) in enumerate(windows):
        for s in range(ATTN_QB):
            n = i * ATTN_QB + s
            variant = jnp.where(n == 0, BIAS_FIRST, jnp.where(n == nblk - 1, BIAS_LAST, BIAS_INTERIOR))
            bias4 = bias_ref[variant, kh * Q_GROUP:(kh + 1) * Q_GROUP].reshape(Q_GROUP * BLOCK, 3 * BLOCK)
            rows = slice(s * BLOCK, (s + 1) * BLOCK)
            q4 = jnp.concatenate(
                [q_ref[rows, (kh * Q_GROUP + g) * HEAD_DIM:(kh * Q_GROUP + g + 1) * HEAD_DIM]
                 for g in range(Q_GROUP)], axis=0)
            k_sub = k_win[s * BLOCK:s * BLOCK + 3 * BLOCK]
            v_ext = jnp.concatenate([v_win[s * BLOCK:s * BLOCK + 3 * BLOCK], ones], axis=1)
            sc = lax.dot_general(q4, k_sub, (((1,), (1,)), ((), ())), preferred_element_type=F32) + bias4
            es, sink_terms = [], []
            for g in range(Q_GROUP):
                sg = sc[g * BLOCK:(g + 1) * BLOCK]
                sink2 = sink_ref[layer, kh * Q_GROUP + g] * LOG2E
                m = jnp.maximum(jnp.max(sg, axis=-1, keepdims=True), sink2)
                es.append(jnp.exp2(sg - m).astype(BF16))
                sink_terms.append(jnp.exp2(sink2 - m))
            pv = jnp.dot(jnp.concatenate(es, axis=0), v_ext, preferred_element_type=F32)
            for g in range(Q_GROUP):
                hd = kh * Q_GROUP + g
                pg = pv[g * BLOCK:(g + 1) * BLOCK]
                o = pg[:, :HEAD_DIM] / (pg[:, HEAD_DIM:] + sink_terms[g])
                o_ref[rows, hd * HEAD_DIM:(hd + 1) * HEAD_DIM] = o.astype(o_ref.dtype)


def _attention(z3, bias, sink, layer):
    b, s, _ = z3.shape
    nblk = s // BLOCK
    assert nblk >= 2 and nblk % ATTN_QB == 0
    tq = ATTN_QB * BLOCK
    kern = functools.partial(_attn_kernel, nblk=nblk, layer=layer)
    return pl.pallas_call(
        kern,
        grid=(b, s // tq),
        in_specs=[
            pl.BlockSpec(memory_space=pltpu.SMEM),
            pl.BlockSpec((None, tq, ATTN_WIDTH), lambda bi, i: (bi, i, 0)),
            pl.BlockSpec((None, BLOCK, COL), lambda bi, i: (bi, jnp.maximum(i * ATTN_QB - 1, 0), COL_KV)),
            pl.BlockSpec((None, tq, COL), lambda bi, i: (bi, i, COL_KV)),
            pl.BlockSpec((None, BLOCK, COL),
                         lambda bi, i: (bi, jnp.minimum((i + 1) * ATTN_QB, nblk - 1), COL_KV)),
            pl.BlockSpec(bias.shape, lambda bi, i: (0, 0, 0, 0)),
        ],
        out_specs=pl.BlockSpec((None, tq, ATTN_WIDTH), lambda bi, i: (bi, i, 0)),
        out_shape=jax.ShapeDtypeStruct((b, s, ATTN_WIDTH), BF16),
        compiler_params=pltpu.CompilerParams(
            dimension_semantics=("arbitrary", "arbitrary"), vmem_limit_bytes=VMEM_LIMIT),
        name="attn",
    )(sink, z3, z3, z3, z3, bias)


def _segment_permutation(t):
    groups = t // SUBLANES
    i = np.arange(t)
    p = np.zeros((t, t), np.float32)
    p[i, (i % SUBLANES) * groups + i // SUBLANES] = 1.0
    return p


def _lru_gates(xc, log_a_scale, wai_ref, ba_ref, bi_ref, d):
    ra, ri = [], []
    for hd in range(LRU_HEADS):
        xh = xc[:, hd * LRU_BLOCK:(hd + 1) * LRU_BLOCK].astype(BF16)
        y = jnp.dot(xh, wai_ref[d, hd], preferred_element_type=F32)
        ra.append(y[:, :LRU_BLOCK])
        ri.append(y[:, LRU_BLOCK:])
    r = _sigmoid(jnp.concatenate(ra, axis=1) + ba_ref[d:d + 1, :])
    gi = _sigmoid(jnp.concatenate(ri, axis=1) + bi_ref[d:d + 1, :])
    log_a = log_a_scale * r
    a = jnp.exp(log_a)
    v = -jnp.tanh(log_a) * (a * a + 1.0)
    root = jnp.where(v > 0.0, v * lax.rsqrt(v), 0.0)
    return a, root * (gi * xc)


def _lru_direction(x3_scr, halo, carry, a_scr, u_scr, out_ref,
                   lw_ref, lb_ref, wai_ref, ba_ref, bi_ref, lam_ref, d):
    reverse = d == 1
    t = MIX_T
    groups = t // SUBLANES
    w = LRU_WIDTH
    k = LRU_CONV_K - 1
    nl = -lam_ref[d:d + 1, :]
    log_a_scale = -LRU_C * (jnp.maximum(nl, 0.0) + jnp.log1p(jnp.exp(-jnp.abs(nl))))
    x3 = x3_scr[...]
    row = lax.broadcasted_iota(jnp.int32, (1, SUBLANES, w), 1)
    if reverse:
        src = x3[:k]
        wrapped = pltpu.roll(jnp.where(row == 0, halo[...], src), SUBLANES - 1, axis=1)
        ext = jnp.concatenate([x3, wrapped], axis=0)
    else:
        src = x3[groups - k:]
        wrapped = pltpu.roll(jnp.where(row == SUBLANES - 1, halo[...], src), 1, axis=1)
        ext = jnp.concatenate([wrapped, x3], axis=0)
    halo[...] = src
    taps = [lw_ref[d, j:j + 1, :] for j in range(LRU_CONV_K)]
    bias = lb_ref[d:d + 1, :]

    def in_time_order(n):
        return range(n - 1, -1, -1) if reverse else range(n)

    h = p = None
    for ci in in_time_order(groups // MIX_CHUNK):
        g0 = ci * MIX_CHUNK
        xc = bias + sum(taps[j] * ext[g0 + j:g0 + j + MIX_CHUNK] for j in range(LRU_CONV_K))
        a, u = _lru_gates(xc.reshape(MIX_CHUNK * SUBLANES, w), log_a_scale, wai_ref, ba_ref, bi_ref, d)
        a3 = a.reshape(MIX_CHUNK, SUBLANES, w)
        u3 = u.reshape(MIX_CHUNK, SUBLANES, w)
        a_scr[g0:g0 + MIX_CHUNK] = a3
        u_scr[g0:g0 + MIX_CHUNK] = u3
        for g in in_time_order(MIX_CHUNK):
            h = u3[g] if h is None else a3[g] * h + u3[g]
            p = a3[g] if p is None else a3[g] * p

    c = carry[0:1, :]
    cs = [None] * SUBLANES
    for r in in_time_order(SUBLANES):
        cs[r] = c
        c = p[r:r + 1, :] * c + h[r:r + 1, :]
    carry[0:1, :] = c

    h = jnp.concatenate(cs, axis=0)
    for g in in_time_order(groups):
        h = a_scr[g] * h + u_scr[g]
        for j in range(w // LANES):
            out_ref[j, pl.ds(g, SUBLANES, stride=SEG_PITCH), :] = h[:, j * LANES:(j + 1) * LANES]
    for r in range(SUBLANES):
        out_ref[:, r * SEG_PITCH + SEG_ROWS:(r + 1) * SEG_PITCH, :] = jnp.zeros(
            (w // LANES, SEG_PITCH - SEG_ROWS, LANES), F32)


def _mixers_kernel(xf_ref, xb_ref, xfn_ref, xbn_ref, perm_ref, lw_ref, lb_ref, wai_ref, ba_ref, bi_ref, lam_ref,
                   hf_ref, hb_ref, x3_f, x3_b, halo_f, halo_b, carry_f, carry_b, a_f, u_f, a_b, u_b):
    def permuted(x_ref):
        x3 = jnp.dot(perm_ref[...], x_ref[...], preferred_element_type=F32)
        return x3.reshape(MIX_T // SUBLANES, SUBLANES, LRU_WIDTH)

    @pl.when(pl.program_id(1) == 0)
    def _():
        halo_f[...] = jnp.zeros(halo_f.shape, F32)
        halo_b[...] = jnp.zeros(halo_b.shape, F32)
        carry_f[...] = jnp.zeros(carry_f.shape, F32)
        carry_b[...] = jnp.zeros(carry_b.shape, F32)

    @pl.when((pl.program_id(0) == 0) & (pl.program_id(1) == 0))
    def _():
        x3_f[...] = permuted(xf_ref)
        x3_b[...] = permuted(xb_ref)

    lru = (lw_ref, lb_ref, wai_ref, ba_ref, bi_ref, lam_ref)
    _lru_direction(x3_f, halo_f, carry_f, a_f, u_f, hf_ref, *lru, 0)
    _lru_direction(x3_b, halo_b, carry_b, a_b, u_b, hb_ref, *lru, 1)
    x3_f[...] = permuted(xfn_ref)
    x3_b[...] = permuted(xbn_ref)


def _mixers(z3, lru_conv_w, lru_conv_b, wai, b_a, b_i, lam, l):
    b, s, _ = z3.shape
    t = MIX_T
    nt = s // t
    slabs = LRU_WIDTH // LANES
    perm = jnp.asarray(_segment_permutation(t), BF16)
    params = (lru_conv_w, lru_conv_b, wai, b_a, b_i, lam)

    def next_tile(bi, i, reverse):
        step = jnp.minimum(bi * nt + i + 1, b * nt - 1)
        ni = step % nt
        return (step // nt, nt - 1 - ni if reverse else ni, COL_LRU_X)

    return pl.pallas_call(
        _mixers_kernel,
        grid=(b, nt),
        in_specs=[
            pl.BlockSpec((None, t, COL), lambda bi, i: (bi, i, COL_LRU_X)),
            pl.BlockSpec((None, t, COL), lambda bi, i: (bi, nt - 1 - i, COL_LRU_X)),
            pl.BlockSpec((None, t, COL), lambda bi, i: next_tile(bi, i, False)),
            pl.BlockSpec((None, t, COL), lambda bi, i: next_tile(bi, i, True)),
            pl.BlockSpec((t, t), lambda bi, i: (0, 0)),
        ] + [_layer_resident(x, l) for x in params],
        out_specs=[
            pl.BlockSpec((slabs, None, TILE_PITCH, LANES), lambda bi, i: (0, bi * nt + i, 0, 0)),
            pl.BlockSpec((slabs, None, TILE_PITCH, LANES), lambda bi, i: (0, bi * nt + nt - 1 - i, 0, 0)),
        ],
        out_shape=[
            jax.ShapeDtypeStruct((slabs, b * nt, TILE_PITCH, LANES), F32),
            jax.ShapeDtypeStruct((slabs, b * nt, TILE_PITCH, LANES), F32),
        ],
        scratch_shapes=[pltpu.VMEM((t // SUBLANES, SUBLANES, LRU_WIDTH), F32)] * 2 + [
            pltpu.VMEM((LRU_CONV_K - 1, SUBLANES, LRU_WIDTH), F32),
            pltpu.VMEM((LRU_CONV_K - 1, SUBLANES, LRU_WIDTH), F32),
            pltpu.VMEM((SUBLANES, LRU_WIDTH), F32),
            pltpu.VMEM((SUBLANES, LRU_WIDTH), F32),
        ] + [pltpu.VMEM((t // SUBLANES, SUBLANES, LRU_WIDTH), F32)] * 4,
        compiler_params=pltpu.CompilerParams(
            dimension_semantics=("arbitrary", "arbitrary"), vmem_limit_bytes=VMEM_LIMIT),
        name="mixers",
    )(z3, z3, z3, z3, perm, *params)


def _outple_kernel(h_ref, ya_ref, hf_ref, hb_ref, ga0_ref, ga1_ref, cb_ref, cc_ref, cx_ref, gc_ref, gl_ref,
                   ccp_ref, cxp_ref, ccn_ref, cxn_ref, p_ref,
                   cw_ref, wo_ref, pn_ref, wg_ref, wp_ref, fn_ref, o_ref, bufc, *, nt, last):
    i = pl.program_id(0)
    pad = SUBLANES
    tm = OUT_TM
    at_start = i % nt == 0
    at_end = i % nt == nt - 1
    before = (ccp_ref[...].astype(F32) * cxp_ref[...].astype(F32))[pad:]
    after = (ccn_ref[...].astype(F32) * cxn_ref[...].astype(F32))[:pad]
    bufc[0:pad, :] = jnp.where(at_start, 0.0, before)
    bufc[pad + tm:2 * pad + tm, :] = jnp.where(at_end, 0.0, after)
    chunk = MIX_CHUNK * SUBLANES
    for r0 in range(0, tm, chunk):
        rows = slice(r0, r0 + chunk)
        bufc[pad + r0:pad + r0 + chunk, :] = cc_ref[rows, :].astype(F32) * cx_ref[rows, :].astype(F32)

    subs = [slice(s * OUT_SUB, (s + 1) * OUT_SUB) for s in range(OUT_TM // OUT_SUB)]
    h1s, pps = [], []
    for rows in subs:
        g_attn = jnp.concatenate([ga0_ref[rows, :], ga1_ref[rows, :]], axis=1).astype(F32)
        y_attn = (ya_ref[rows, :].astype(F32) * _silu(g_attn)).astype(BF16)
        segs = range(rows.start // SEG_ROWS, rows.stop // SEG_ROWS)
        y_lru = jnp.concatenate(
            [jnp.concatenate([hf_ref[j, r * SEG_PITCH:r * SEG_PITCH + SEG_ROWS, :]
                              + hb_ref[j, r * SEG_PITCH:r * SEG_PITCH + SEG_ROWS, :] for r in segs], axis=0)
             for j in range(LRU_WIDTH // LANES)], axis=1)
        y_lru = (y_lru * _silu(gl_ref[rows, :].astype(F32))).astype(BF16)
        conv = sum(cw_ref[j:j + 1, :] * bufc[pad - 1 + j + rows.start:pad - 1 + j + rows.stop, :]
                   for j in range(CONV_K))
        y_conv = (cb_ref[rows, :].astype(F32) * conv * _silu(gc_ref[rows, :].astype(F32))).astype(BF16)
        proj = (jnp.dot(y_attn, wo_ref[0:ATTN_WIDTH, :], preferred_element_type=F32)
                + jnp.dot(y_lru, wo_ref[ATTN_WIDTH + CONV_WIDTH:, :], preferred_element_type=F32)
                + jnp.dot(y_conv, wo_ref[ATTN_WIDTH:ATTN_WIDTH + CONV_WIDTH, :], preferred_element_type=F32))
        h1s.append(h_ref[rows, :] + proj)
    for rows in subs:
        pps.append(jnp.dot(p_ref[rows, :].astype(BF16), wp_ref[...], preferred_element_type=F32))
    for rows, h1, pp in zip(subs, h1s, pps):
        n = _rms(h1, pn_ref[...]).astype(BF16)
        gate = _sigmoid(jnp.dot(n, wg_ref[...], preferred_element_type=F32))
        h2 = h1 + gate * pp
        if last:
            h2 = _rms(h2, fn_ref[...])
        o_ref[rows, :] = h2


def _outple(h2d, ya, hf, hb, z2d, p3d, cw, wo_b, pn, wg_b, wp_b, fn, l, nt, last):
    m = h2d.shape[0]
    tm = OUT_TM
    halo = 16
    kern = functools.partial(_outple_kernel, nt=nt, last=last)

    def rows(width, c=0):
        return pl.BlockSpec((tm, width), lambda i: (i, c))

    def halo_before(c):
        return pl.BlockSpec((halo, COL), lambda i: (jnp.maximum(i * (tm // halo) - 1, 0), c))

    def halo_after(c):
        return pl.BlockSpec((halo, COL), lambda i: (jnp.minimum((i + 1) * (tm // halo), m // halo - 1), c))

    assert tm == MIX_T and OUT_SUB % SEG_ROWS == 0
    slab = pl.BlockSpec((LRU_WIDTH // LANES, None, TILE_PITCH, LANES), lambda i: (0, i, 0, 0))
    return pl.pallas_call(
        kern,
        grid=(m // tm,),
        in_specs=[
            rows(D_MODEL), rows(ATTN_WIDTH), slab, slab,
            rows(COL, COL_G_ATTN), rows(COL, COL_G_ATTN + 1), rows(COL, COL_CONV_B), rows(COL, COL_CONV_C),
            rows(COL, COL_CONV_X), rows(COL, COL_G_CONV), rows(COL, COL_G_LRU),
            halo_before(COL_CONV_C), halo_before(COL_CONV_X), halo_after(COL_CONV_C), halo_after(COL_CONV_X),
            pl.BlockSpec((None, tm, PLE_DIM), lambda i: (l, i, 0)),
            _layer_resident(cw, l), _layer_resident(wo_b, l), _layer_resident(pn, l), _layer_resident(wg_b, l),
            _layer_resident(wp_b, l), _layer_resident(fn, 0),
        ],
        out_specs=rows(D_MODEL),
        out_shape=jax.ShapeDtypeStruct((m, D_MODEL), F32),
        scratch_shapes=[pltpu.VMEM((tm + 2 * SUBLANES, CONV_WIDTH), F32)],
        compiler_params=pltpu.CompilerParams(
            dimension_semantics=("arbitrary",), vmem_limit_bytes=VMEM_LIMIT),
        name="outple",
    )(h2d, ya, hf, hb, *([z2d] * 11), p3d, cw, wo_b, pn, wg_b, wp_b, fn)


def _trunk(x, p, prm, bias):
    b, s, _ = x.shape
    m = b * s
    depth = p.shape[0]
    h = x.reshape(m, D_MODEL)
    p3d = p.reshape(depth, m, PLE_DIM)
    for l in range(depth):
        z = _inproj(h, prm["norm_mix"], prm["w_in"], l)
        z3 = z.reshape(b, s, IN_WIDTH)
        ya = _attention(z3, bias, prm["attn_sink"], l)
        hf, hb = _mixers(z3, prm["lru_conv_w"], prm["lru_conv_b"],
                         prm["wai"], prm["lru_b_a"], prm["lru_b_i"], prm["lru_L"], l)
        h = _outple(h, ya.reshape(m, ATTN_WIDTH), hf, hb, z, p3d,
                    prm["conv_w"], prm["w_out"], prm["ple_norm"], prm["ple_w_gate"], prm["ple_w_proj"],
                    prm["final_norm"], l, s // OUT_TM, last=(l == depth - 1))
    return h.reshape(b, s, D_MODEL)


def kernel(x_prompt, x_sample, p_prompt, p_sample, norm_mix, w_in, w_out, rel_bias, attn_sink, conv_w,
           lru_conv_w, lru_conv_b, lru_w_a, lru_b_a, lru_w_i, lru_b_i, lru_L, ple_norm, ple_w_gate,
           ple_w_proj, final_norm):
    col_scale = jnp.where(jnp.arange(IN_WIDTH) < ATTN_WIDTH, Q_SCALE, 1.0).astype(F32)
    prm = dict(
        norm_mix=norm_mix[:, None, :], w_in=(w_in * col_scale).astype(BF16), w_out=w_out.astype(BF16),
        attn_sink=attn_sink, conv_w=conv_w, lru_conv_w=lru_conv_w, lru_conv_b=lru_conv_b,
        wai=jnp.concatenate([lru_w_a, lru_w_i], axis=-1).astype(BF16),
        lru_b_a=lru_b_a, lru_b_i=lru_b_i, lru_L=lru_L, ple_norm=ple_norm[:, None, :],
        ple_w_gate=ple_w_gate.astype(BF16), ple_w_proj=ple_w_proj.astype(BF16),
        final_norm=final_norm[None, None, :])
    bias = _rel_bias(rel_bias)
    y_prompt = _trunk(x_prompt, p_prompt, prm, bias)
    y_sample = _trunk(x_sample, p_sample, prm, bias)
    return (y_prompt, y_sample)
```

```python
import functools
import math

import numpy as np
import jax
import jax.numpy as jnp
from jax import lax
from jax.experimental import pallas as pl
from jax.experimental.pallas import tpu as pltpu

F32 = jnp.float32
BF16 = jnp.bfloat16

D_MODEL = 2048
N_HEADS = 8
N_KV_HEADS = 2
Q_GROUP = N_HEADS // N_KV_HEADS
HEAD_DIM = 128
ATTN_WIDTH = N_HEADS * HEAD_DIM
KV_WIDTH = N_KV_HEADS * HEAD_DIM
WINDOW = 128
BLOCK = 128
N_BUCKETS = 32
MAX_DISTANCE = 128
CONV_WIDTH = 512
CONV_K = 3
LRU_WIDTH = 512
LRU_HEADS = 4
LRU_BLOCK = 128
LRU_CONV_K = 4
LRU_C = 8.0
MIX_WIDTH = ATTN_WIDTH + CONV_WIDTH + LRU_WIDTH
IN_WIDTH = 2 * ATTN_WIDTH + 2 * KV_WIDTH + 4 * CONV_WIDTH + 2 * LRU_WIDTH
PLE_DIM = 256
NORM_EPS = 1e-6
NEG_INF = -1e30
LOG2E = math.log2(math.e)
Q_SCALE = HEAD_DIM ** -0.5 * LOG2E

COL = 512
COL_KV = 2
COL_G_ATTN = 3
COL_CONV_B = 5
COL_CONV_C = 6
COL_CONV_X = 7
COL_G_CONV = 8
COL_LRU_X = 9
COL_G_LRU = 10

SUBLANES = 8
LANES = 128
VMEM_LIMIT = 60 * 1024 * 1024

INPROJ_TM = 512
INPROJ_TN = 512
ATTN_QB = 4
MIX_T = 512
MIX_CHUNK = 8
SEG_ROWS = MIX_T // SUBLANES
SEG_PITCH = SEG_ROWS + SUBLANES
TILE_PITCH = SUBLANES * SEG_PITCH
OUT_TM = 512
OUT_SUB = 256


def _sigmoid(x):
    return 1.0 / (1.0 + jnp.exp2(x * (-LOG2E)))


def _silu(x):
    return x * _sigmoid(x)


def _rms(x, g):
    ms = jnp.mean(x * x, axis=-1, keepdims=True)
    return x * lax.rsqrt(ms + NORM_EPS) * g


def _layer_resident(x, l):
    nd = x.ndim - 1
    return pl.BlockSpec((None,) + x.shape[1:], lambda *_: (l,) + (0,) * nd, pipeline_mode=pl.Buffered(1))


def _inproj_kernel(h_ref, g_ref, w_ref, z_ref):
    u = _rms(h_ref[...], g_ref[...]).astype(BF16)
    for c in range(IN_WIDTH // INPROJ_TN):
        cols = slice(c * INPROJ_TN, (c + 1) * INPROJ_TN)
        z_ref[:, cols] = jnp.dot(u, w_ref[:, cols], preferred_element_type=F32).astype(z_ref.dtype)


def _inproj(h2d, g, w_b, l):
    m = h2d.shape[0]
    tm = INPROJ_TM
    return pl.pallas_call(
        _inproj_kernel,
        grid=(m // tm,),
        in_specs=[
            pl.BlockSpec((tm, D_MODEL), lambda i: (i, 0)),
            _layer_resident(g, l),
            _layer_resident(w_b, l),
        ],
        out_specs=pl.BlockSpec((tm, IN_WIDTH), lambda i: (i, 0)),
        out_shape=jax.ShapeDtypeStruct((m, IN_WIDTH), BF16),
        compiler_params=pltpu.CompilerParams(
            dimension_semantics=("arbitrary",), vmem_limit_bytes=VMEM_LIMIT),
        name="inproj",
    )(h2d, g, w_b)


def _bucket_table():
    q = np.arange(BLOCK)[:, None]
    c = np.arange(3 * BLOCK)[None, :]
    rel = c - BLOCK - q
    n = np.abs(rel)
    half = N_BUCKETS // 2
    max_exact = half // 2
    n2 = np.maximum(n, 1).astype(np.int64) ** 2
    floor_log2 = np.floor(np.log2(n2.astype(np.float64)) + 1e-9).astype(np.int64)
    large = np.minimum(max_exact + floor_log2 - 6, half - 1)
    bucket = np.where(rel > 0, half, 0) + np.where(n < max_exact, n, large)
    bucket = np.where(n <= WINDOW, bucket, -1)
    return bucket.astype(np.int32)


BIAS_INTERIOR, BIAS_FIRST, BIAS_LAST = 0, 1, 2


def _bias_kernel(tab_ref, bkt_ref, o_ref):
    h = pl.program_id(0)
    b = bkt_ref[...]
    acc = jnp.full(b.shape, NEG_INF, F32)
    for k in range(N_BUCKETS):
        acc = jnp.where(b == k, tab_ref[k, h] * LOG2E, acc)
    col = lax.broadcasted_iota(jnp.int32, b.shape, 1)
    o_ref[BIAS_INTERIOR, 0] = acc
    o_ref[BIAS_FIRST, 0] = jnp.where(col < BLOCK, NEG_INF, acc)
    o_ref[BIAS_LAST, 0] = jnp.where(col >= 2 * BLOCK, NEG_INF, acc)


def _rel_bias(rel_table):
    bkt = jnp.asarray(_bucket_table())
    return pl.pallas_call(
        _bias_kernel,
        grid=(N_HEADS,),
        in_specs=[
            pl.BlockSpec(memory_space=pltpu.SMEM),
            pl.BlockSpec((BLOCK, 3 * BLOCK), lambda h: (0, 0)),
        ],
        out_specs=pl.BlockSpec((3, 1, BLOCK, 3 * BLOCK), lambda h: (0, h, 0, 0)),
        out_shape=jax.ShapeDtypeStruct((3, N_HEADS, BLOCK, 3 * BLOCK), F32),
        name="relbias",
    )(rel_table, bkt)


def _attn_kernel(sink_ref, q_ref, kvp_ref, kvc_ref, kvn_ref, bias_ref, o_ref, *, nblk, layer):
    i = pl.program_id(1)
    ones = jnp.ones((3 * BLOCK, HEAD_DIM), BF16)
    windows = []
    for kh in range(N_KV_HEADS):
        ks = slice(kh * HEAD_DIM, (kh + 1) * HEAD_DIM)
        vs = slice(KV_WIDTH + kh * HEAD_DIM, KV_WIDTH + (kh + 1) * HEAD_DIM)
        windows.append((jnp.concatenate([kvp_ref[:, ks], kvc_ref[:, ks], kvn_ref[:, ks]], axis=0),
                        jnp.concatenate([kvp_ref[:, vs], kvc_ref[:, vs], kvn_ref[:, vs]], axis=0)))
    for kh, (k_win, v_win) in enumerate(windows):
        for s in range(ATTN_QB):
            n = i * ATTN_QB + s
            variant = jnp.where(n == 0, BIAS_FIRST, jnp.where(n == nblk - 1, BIAS_LAST, BIAS_INTERIOR))
            bias4 = bias_ref[variant, kh * Q_GROUP:(kh + 1) * Q_GROUP].reshape(Q_GROUP * BLOCK, 3 * BLOCK)
            rows = slice(s * BLOCK, (s + 1) * BLOCK)
            group_cols = slice(kh * Q_GROUP * HEAD_DIM, (kh + 1) * Q_GROUP * HEAD_DIM)
            q_blk = q_ref[rows, group_cols]
            q4 = jnp.concatenate([q_blk[:, g * HEAD_DIM:(g + 1) * HEAD_DIM] for g in range(Q_GROUP)], axis=0)
            k_sub = k_win[s * BLOCK:s * BLOCK + 3 * BLOCK]
            v_ext = jnp.concatenate([v_win[s * BLOCK:s * BLOCK + 3 * BLOCK], ones], axis=1)
            sc = lax.dot_general(q4, k_sub, (((1,), (1,)), ((), ())), preferred_element_type=F32) + bias4
            es, sink_terms = [], []
            for g in range(Q_GROUP):
                sg = sc[g * BLOCK:(g + 1) * BLOCK]
                sink2 = sink_ref[layer, kh * Q_GROUP + g] * LOG2E
                m = jnp.maximum(jnp.max(sg, axis=-1, keepdims=True), sink2)
                es.append(jnp.exp2(sg - m).astype(BF16))
                sink_terms.append(jnp.exp2(sink2 - m))
            pv = jnp.dot(jnp.concatenate(es, axis=0), v_ext, preferred_element_type=F32)
            outs = []
            for g in range(Q_GROUP):
                pg = pv[g * BLOCK:(g + 1) * BLOCK]
                outs.append(pg[:, :HEAD_DIM] / (pg[:, HEAD_DIM:] + sink_terms[g]))
            o_ref[rows, group_cols] = jnp.concatenate(outs, axis=1).astype(o_ref.dtype)


def _attention(z3, bias, sink, layer):
    b, s, _ = z3.shape
    nblk = s // BLOCK
    assert nblk >= 2 and nblk % ATTN_QB == 0
    tq = ATTN_QB * BLOCK
    kern = functools.partial(_attn_kernel, nblk=nblk, layer=layer)
    return pl.pallas_call(
        kern,
        grid=(b, s // tq),
        in_specs=[
            pl.BlockSpec(memory_space=pltpu.SMEM),
            pl.BlockSpec((None, tq, ATTN_WIDTH), lambda bi, i: (bi, i, 0)),
            pl.BlockSpec((None, BLOCK, COL), lambda bi, i: (bi, jnp.maximum(i * ATTN_QB - 1, 0), COL_KV)),
            pl.BlockSpec((None, tq, COL), lambda bi, i: (bi, i, COL_KV)),
            pl.BlockSpec((None, BLOCK, COL),
                         lambda bi, i: (bi, jnp.minimum((i + 1) * ATTN_QB, nblk - 1), COL_KV)),
            pl.BlockSpec(bias.shape, lambda bi, i: (0, 0, 0, 0)),
        ],
        out_specs=pl.BlockSpec((None, tq, ATTN_WIDTH), lambda bi, i: (bi, i, 0)),
        out_shape=jax.ShapeDtypeStruct((b, s, ATTN_WIDTH), BF16),
        compiler_params=pltpu.CompilerParams(
            dimension_semantics=("arbitrary", "arbitrary"), vmem_limit_bytes=VMEM_LIMIT),
        name="attn",
    )(sink, z3, z3, z3, z3, bias)


def _segment_permutation(t):
    groups = t // SUBLANES
    i = np.arange(t)
    p = np.zeros((t, t), np.float32)
    p[i, (i % SUBLANES) * groups + i // SUBLANES] = 1.0
    return p


def _lru_gates(xc, log_a_scale, wai_ref, ba_ref, bi_ref, d):
    ra, ri = [], []
    for hd in range(LRU_HEADS):
        xh = xc[:, hd * LRU_BLOCK:(hd + 1) * LRU_BLOCK].astype(BF16)
        y = jnp.dot(xh, wai_ref[d, hd], preferred_element_type=F32)
        ra.append(y[:, :LRU_BLOCK])
        ri.append(y[:, LRU_BLOCK:])
    r = _sigmoid(jnp.concatenate(ra, axis=1) + ba_ref[d:d + 1, :])
    gi = _sigmoid(jnp.concatenate(ri, axis=1) + bi_ref[d:d + 1, :])
    log_a = log_a_scale * r
    a = jnp.exp(log_a)
    v = -jnp.tanh(log_a) * (a * a + 1.0)
    root = jnp.where(v > 0.0, v * lax.rsqrt(v), 0.0)
    return a, root * (gi * xc)


def _lru_direction(x3_scr, halo, carry, a_scr, u_scr, out_ref,
                   lw_ref, lb_ref, wai_ref, ba_ref, bi_ref, lam_ref, d):
    reverse = d == 1
    t = MIX_T
    groups = t // SUBLANES
    w = LRU_WIDTH
    k = LRU_CONV_K - 1
    nl = -lam_ref[d:d + 1, :]
    log_a_scale = -LRU_C * (jnp.maximum(nl, 0.0) + jnp.log1p(jnp.exp(-jnp.abs(nl))))
    x3 = x3_scr[...]
    row = lax.broadcasted_iota(jnp.int32, (1, SUBLANES, w), 1)
    if reverse:
        src = x3[:k]
        wrapped = pltpu.roll(jnp.where(row == 0, halo[...], src), SUBLANES - 1, axis=1)
        ext = jnp.concatenate([x3, wrapped], axis=0)
    else:
        src = x3[groups - k:]
        wrapped = pltpu.roll(jnp.where(row == SUBLANES - 1, halo[...], src), 1, axis=1)
        ext = jnp.concatenate([wrapped, x3], axis=0)
    halo[...] = src
    taps = [lw_ref[d, j:j + 1, :] for j in range(LRU_CONV_K)]
    bias = lb_ref[d:d + 1, :]

    def in_time_order(n):
        return range(n - 1, -1, -1) if reverse else range(n)

    h = p = None
    for ci in in_time_order(groups // MIX_CHUNK):
        g0 = ci * MIX_CHUNK
        xc = bias + sum(taps[j] * ext[g0 + j:g0 + j + MIX_CHUNK] for j in range(LRU_CONV_K))
        a, u = _lru_gates(xc.reshape(MIX_CHUNK * SUBLANES, w), log_a_scale, wai_ref, ba_ref, bi_ref, d)
        a3 = a.reshape(MIX_CHUNK, SUBLANES, w)
        u3 = u.reshape(MIX_CHUNK, SUBLANES, w)
        a_scr[g0:g0 + MIX_CHUNK] = a3
        u_scr[g0:g0 + MIX_CHUNK] = u3
        for g in in_time_order(MIX_CHUNK):
            h = u3[g] if h is None else a3[g] * h + u3[g]
            p = a3[g] if p is None else a3[g] * p

    c = carry[0:1, :]
    cs = [None] * SUBLANES
    for r in in_time_order(SUBLANES):
        cs[r] = c
        c = p[r:r + 1, :] * c + h[r:r + 1, :]
    carry[0:1, :] = c

    h = jnp.concatenate(cs, axis=0)
    for g in in_time_order(groups):
        h = a_scr[g] * h + u_scr[g]
        for j in range(w // LANES):
            out_ref[j, pl.ds(g, SUBLANES, stride=SEG_PITCH), :] = h[:, j * LANES:(j + 1) * LANES]
    for r in range(SUBLANES):
        out_ref[:, r * SEG_PITCH + SEG_ROWS:(r + 1) * SEG_PITCH, :] = jnp.zeros(
            (w // LANES, SEG_PITCH - SEG_ROWS, LANES), F32)


def _mixers_kernel(xf_ref, xb_ref, xfn_ref, xbn_ref, perm_ref, lw_ref, lb_ref, wai_ref, ba_ref, bi_ref, lam_ref,
                   hf_ref, hb_ref, x3_f, x3_b, halo_f, halo_b, carry_f, carry_b, a_f, u_f, a_b, u_b):
    def permuted(x_ref):
        x3 = jnp.dot(perm_ref[...], x_ref[...], preferred_element_type=F32)
        return x3.reshape(MIX_T // SUBLANES, SUBLANES, LRU_WIDTH)

    @pl.when(pl.program_id(1) == 0)
    def _():
        halo_f[...] = jnp.zeros(halo_f.shape, F32)
        halo_b[...] = jnp.zeros(halo_b.shape, F32)
        carry_f[...] = jnp.zeros(carry_f.shape, F32)
        carry_b[...] = jnp.zeros(carry_b.shape, F32)

    @pl.when((pl.program_id(0) == 0) & (pl.program_id(1) == 0))
    def _():
        x3_f[...] = permuted(xf_ref)
        x3_b[...] = permuted(xb_ref)

    lru = (lw_ref, lb_ref, wai_ref, ba_ref, bi_ref, lam_ref)
    _lru_direction(x3_f, halo_f, carry_f, a_f, u_f, hf_ref, *lru, 0)
    _lru_direction(x3_b, halo_b, carry_b, a_b, u_b, hb_ref, *lru, 1)
    x3_f[...] = permuted(xfn_ref)
    x3_b[...] = permuted(xbn_ref)


def _mixers(z3, lru_conv_w, lru_conv_b, wai, b_a, b_i, lam, l):
    b, s, _ = z3.shape
    t = MIX_T
    nt = s // t
    slabs = LRU_WIDTH // LANES
    perm = jnp.asarray(_segment_permutation(t), BF16)
    params = (lru_conv_w, lru_conv_b, wai, b_a, b_i, lam)

    def next_tile(bi, i, reverse):
        step = jnp.minimum(bi * nt + i + 1, b * nt - 1)
        ni = step % nt
        return (step // nt, nt - 1 - ni if reverse else ni, COL_LRU_X)

    return pl.pallas_call(
        _mixers_kernel,
        grid=(b, nt),
        in_specs=[
            pl.BlockSpec((None, t, COL), lambda bi, i: (bi, i, COL_LRU_X)),
            pl.BlockSpec((None, t, COL), lambda bi, i: (bi, nt - 1 - i, COL_LRU_X)),
            pl.BlockSpec((None, t, COL), lambda bi, i: next_tile(bi, i, False)),
            pl.BlockSpec((None, t, COL), lambda bi, i: next_tile(bi, i, True)),
            pl.BlockSpec((t, t), lambda bi, i: (0, 0)),
        ] + [_layer_resident(x, l) for x in params],
        out_specs=[
            pl.BlockSpec((slabs, None, TILE_PITCH, LANES), lambda bi, i: (0, bi * nt + i, 0, 0)),
            pl.BlockSpec((slabs, None, TILE_PITCH, LANES), lambda bi, i: (0, bi * nt + nt - 1 - i, 0, 0)),
        ],
        out_shape=[
            jax.ShapeDtypeStruct((slabs, b * nt, TILE_PITCH, LANES), F32),
            jax.ShapeDtypeStruct((slabs, b * nt, TILE_PITCH, LANES), F32),
        ],
        scratch_shapes=[pltpu.VMEM((t // SUBLANES, SUBLANES, LRU_WIDTH), F32)] * 2 + [
            pltpu.VMEM((LRU_CONV_K - 1, SUBLANES, LRU_WIDTH), F32),
            pltpu.VMEM((LRU_CONV_K - 1, SUBLANES, LRU_WIDTH), F32),
            pltpu.VMEM((SUBLANES, LRU_WIDTH), F32),
            pltpu.VMEM((SUBLANES, LRU_WIDTH), F32),
        ] + [pltpu.VMEM((t // SUBLANES, SUBLANES, LRU_WIDTH), F32)] * 4,
        compiler_params=pltpu.CompilerParams(
            dimension_semantics=("arbitrary", "arbitrary"), vmem_limit_bytes=VMEM_LIMIT),
        name="mixers",
    )(z3, z3, z3, z3, perm, *params)


def _outple_kernel(h_ref, ya_ref, hf_ref, hb_ref, ga0_ref, ga1_ref, cb_ref, cc_ref, cx_ref, gc_ref, gl_ref,
                   ccp_ref, cxp_ref, ccn_ref, cxn_ref, p_ref,
                   cw_ref, wo_ref, pn_ref, wg_ref, wp_ref, fn_ref, o_ref, bufc, *, nt, last):
    i = pl.program_id(0)
    pad = SUBLANES
    tm = OUT_TM
    at_start = i % nt == 0
    at_end = i % nt == nt - 1
    before = (ccp_ref[...].astype(F32) * cxp_ref[...].astype(F32))[pad:]
    after = (ccn_ref[...].astype(F32) * cxn_ref[...].astype(F32))[:pad]
    bufc[0:pad, :] = jnp.where(at_start, 0.0, before)
    bufc[pad + tm:2 * pad + tm, :] = jnp.where(at_end, 0.0, after)
    chunk = MIX_CHUNK * SUBLANES
    for r0 in range(0, tm, chunk):
        rows = slice(r0, r0 + chunk)
        bufc[pad + r0:pad + r0 + chunk, :] = cc_ref[rows, :].astype(F32) * cx_ref[rows, :].astype(F32)

    subs = [slice(s * OUT_SUB, (s + 1) * OUT_SUB) for s in range(OUT_TM // OUT_SUB)]
    h1s, pps = [], []
    for rows in subs:
        g_attn = jnp.concatenate([ga0_ref[rows, :], ga1_ref[rows, :]], axis=1).astype(F32)
        y_attn = (ya_ref[rows, :].astype(F32) * _silu(g_attn)).astype(BF16)
        segs = range(rows.start // SEG_ROWS, rows.stop // SEG_ROWS)
        y_lru = jnp.concatenate(
            [jnp.concatenate([hf_ref[j, r * SEG_PITCH:r * SEG_PITCH + SEG_ROWS, :]
                              + hb_ref[j, r * SEG_PITCH:r * SEG_PITCH + SEG_ROWS, :] for r in segs], axis=0)
             for j in range(LRU_WIDTH // LANES)], axis=1)
        y_lru = (y_lru * _silu(gl_ref[rows, :].astype(F32))).astype(BF16)
        conv = sum(cw_ref[j:j + 1, :] * bufc[pad - 1 + j + rows.start:pad - 1 + j + rows.stop, :]
                   for j in range(CONV_K))
        y_conv = (cb_ref[rows, :].astype(F32) * conv * _silu(gc_ref[rows, :].astype(F32))).astype(BF16)
        proj = (jnp.dot(y_attn, wo_ref[0:ATTN_WIDTH, :], preferred_element_type=F32)
                + jnp.dot(y_lru, wo_ref[ATTN_WIDTH + CONV_WIDTH:, :], preferred_element_type=F32)
                + jnp.dot(y_conv, wo_ref[ATTN_WIDTH:ATTN_WIDTH + CONV_WIDTH, :], preferred_element_type=F32))
        h1s.append(h_ref[rows, :] + proj)
    for rows in subs:
        pps.append(jnp.dot(p_ref[rows, :].astype(BF16), wp_ref[...], preferred_element_type=F32))
    for rows, h1, pp in zip(subs, h1s, pps):
        n = _rms(h1, pn_ref[...]).astype(BF16)
        gate = _sigmoid(jnp.dot(n, wg_ref[...], preferred_element_type=F32))
        h2 = h1 + gate * pp
        if last:
            h2 = _rms(h2, fn_ref[...])
        o_ref[rows, :] = h2


def _outple(h2d, ya, hf, hb, z2d, p3d, cw, wo_b, pn, wg_b, wp_b, fn, l, nt, last):
    m = h2d.shape[0]
    tm = OUT_TM
    halo = 16
    kern = functools.partial(_outple_kernel, nt=nt, last=last)

    def rows(width, c=0):
        return pl.BlockSpec((tm, width), lambda i: (i, c))

    def halo_before(c):
        return pl.BlockSpec((halo, COL), lambda i: (jnp.maximum(i * (tm // halo) - 1, 0), c))

    def halo_after(c):
        return pl.BlockSpec((halo, COL), lambda i: (jnp.minimum((i + 1) * (tm // halo), m // halo - 1), c))

    assert tm == MIX_T and OUT_SUB % SEG_ROWS == 0
    slab = pl.BlockSpec((LRU_WIDTH // LANES, None, TILE_PITCH, LANES), lambda i: (0, i, 0, 0))
    return pl.pallas_call(
        kern,
        grid=(m // tm,),
        in_specs=[
            rows(D_MODEL), rows(ATTN_WIDTH), slab, slab,
            rows(COL, COL_G_ATTN), rows(COL, COL_G_ATTN + 1), rows(COL, COL_CONV_B), rows(COL, COL_CONV_C),
            rows(COL, COL_CONV_X), rows(COL, COL_G_CONV), rows(COL, COL_G_LRU),
            halo_before(COL_CONV_C), halo_before(COL_CONV_X), halo_after(COL_CONV_C), halo_after(COL_CONV_X),
            pl.BlockSpec((None, tm, PLE_DIM), lambda i: (l, i, 0)),
            _layer_resident(cw, l), _layer_resident(wo_b, l), _layer_resident(pn, l), _layer_resident(wg_b, l),
            _layer_resident(wp_b, l), _layer_resident(fn, 0),
        ],
        out_specs=rows(D_MODEL),
        out_shape=jax.ShapeDtypeStruct((m, D_MODEL), F32),
        scratch_shapes=[pltpu.VMEM((tm + 2 * SUBLANES, CONV_WIDTH), F32)],
        compiler_params=pltpu.CompilerParams(
            dimension_semantics=("arbitrary",), vmem_limit_bytes=VMEM_LIMIT),
        name="outple",
    )(h2d, ya, hf, hb, *([z2d] * 11), p3d, cw, wo_b, pn, wg_b, wp_b, fn)


def _trunk(x, p, prm, bias):
    b, s, _ = x.shape
    m = b * s
    depth = p.shape[0]
    h = x.reshape(m, D_MODEL)
    p3d = p.reshape(depth, m, PLE_DIM)
    for l in range(depth):
        z = _inproj(h, prm["norm_mix"], prm["w_in"], l)
        z3 = z.reshape(b, s, IN_WIDTH)
        ya = _attention(z3, bias, prm["attn_sink"], l)
        hf, hb = _mixers(z3, prm["lru_conv_w"], prm["lru_conv_b"],
                         prm["wai"], prm["lru_b_a"], prm["lru_b_i"], prm["lru_L"], l)
        h = _outple(h, ya.reshape(m, ATTN_WIDTH), hf, hb, z, p3d,
                    prm["conv_w"], prm["w_out"], prm["ple_norm"], prm["ple_w_gate"], prm["ple_w_proj"],
                    prm["final_norm"], l, s // OUT_TM, last=(l == depth - 1))
    return h.reshape(b, s, D_MODEL)


def kernel(x_prompt, x_sample, p_prompt, p_sample, norm_mix, w_in, w_out, rel_bias, attn_sink, conv_w,
           lru_conv_w, lru_conv_b, lru_w_a, lru_b_a, lru_w_i, lru_b_i, lru_L, ple_norm, ple_w_gate,
           ple_w_proj, final_norm):
    col_scale = jnp.where(jnp.arange(IN_WIDTH) < ATTN_WIDTH, Q_SCALE, 1.0).astype(F32)
    prm = dict(
        norm_mix=norm_mix[:, None, :], w_in=(w_in * col_scale).astype(BF16), w_out=w_out.astype(BF16),
        attn_sink=attn_sink, conv_w=conv_w, lru_conv_w=lru_conv_w, lru_conv_b=lru_conv_b,
        wai=jnp.concatenate([lru_w_a, lru_w_i], axis=-1).astype(BF16),
        lru_b_a=lru_b_a, lru_b_i=lru_b_i, lru_L=lru_L, ple_norm=ple_norm[:, None, :],
        ple_w_gate=ple_w_gate.astype(BF16), ple_w_proj=ple_w_proj.astype(BF16),
        final_norm=final_norm[None, None, :])
    bias = _rel_bias(rel_bias)
    y_prompt = _trunk(x_prompt, p_prompt, prm, bias)
    y_sample = _trunk(x_sample, p_sample, prm, bias)
    return (y_prompt, y_sample)
```

```python
import functools
import math

import numpy as np
import jax
import jax.numpy as jnp
from jax import lax
from jax.experimental import pallas as pl
from jax.experimental.pallas import tpu as pltpu

F32 = jnp.float32
BF16 = jnp.bfloat16

D_MODEL = 2048
N_HEADS = 8
N_KV_HEADS = 2
Q_GROUP = N_HEADS // N_KV_HEADS
HEAD_DIM = 128
ATTN_WIDTH = N_HEADS * HEAD_DIM
KV_WIDTH = N_KV_HEADS * HEAD_DIM
WINDOW = 128
BLOCK = 128
N_BUCKETS = 32
MAX_DISTANCE = 128
CONV_WIDTH = 512
CONV_K = 3
LRU_WIDTH = 512
LRU_HEADS = 4
LRU_BLOCK = 128
LRU_CONV_K = 4
LRU_C = 8.0
MIX_WIDTH = ATTN_WIDTH + CONV_WIDTH + LRU_WIDTH
IN_WIDTH = 2 * ATTN_WIDTH + 2 * KV_WIDTH + 4 * CONV_WIDTH + 2 * LRU_WIDTH
PLE_DIM = 256
NORM_EPS = 1e-6
NEG_INF = -1e30
LOG2E = math.log2(math.e)
Q_SCALE = HEAD_DIM ** -0.5 * LOG2E

COL = 512
COL_KV = 2
COL_G_ATTN = 3
COL_CONV_B = 5
COL_CONV_C = 6
COL_CONV_X = 7
COL_G_CONV = 8
COL_LRU_X = 9
COL_G_LRU = 10

SUBLANES = 8
LANES = 128
VMEM_LIMIT = 60 * 1024 * 1024

INPROJ_TM = 512
INPROJ_TN = 512
ATTN_QB = 4
MIX_T = 512
MIX_CHUNK = 8
SEG_ROWS = MIX_T // SUBLANES
SEG_PITCH = SEG_ROWS + SUBLANES
TILE_PITCH = SUBLANES * SEG_PITCH
OUT_TM = 512
OUT_SUB = 256


def _sigmoid(x):
    return 1.0 / (1.0 + jnp.exp2(x * (-LOG2E)))


def _silu(x):
    return x * _sigmoid(x)


def _rms(x, g):
    ms = jnp.mean(x * x, axis=-1, keepdims=True)
    return x * lax.rsqrt(ms + NORM_EPS) * g


def _layer_resident(x, l):
    nd = x.ndim - 1
    return pl.BlockSpec((None,) + x.shape[1:], lambda *_: (l,) + (0,) * nd, pipeline_mode=pl.Buffered(1))


def _inproj_kernel(h_ref, g_ref, w_ref, z_ref):
    u = _rms(h_ref[...], g_ref[...]).astype(BF16)
    for c in range(IN_WIDTH // INPROJ_TN):
        cols = slice(c * INPROJ_TN, (c + 1) * INPROJ_TN)
        z_ref[:, cols] = jnp.dot(u, w_ref[:, cols], preferred_element_type=F32).astype(z_ref.dtype)


def _inproj(h2d, g, w_b, l):
    m = h2d.shape[0]
    tm = INPROJ_TM
    return pl.pallas_call(
        _inproj_kernel,
        grid=(m // tm,),
        in_specs=[
            pl.BlockSpec((tm, D_MODEL), lambda i: (i, 0)),
            _layer_resident(g, l),
            _layer_resident(w_b, l),
        ],
        out_specs=pl.BlockSpec((tm, IN_WIDTH), lambda i: (i, 0)),
        out_shape=jax.ShapeDtypeStruct((m, IN_WIDTH), BF16),
        compiler_params=pltpu.CompilerParams(
            dimension_semantics=("arbitrary",), vmem_limit_bytes=VMEM_LIMIT),
        name="inproj",
    )(h2d, g, w_b)


def _bucket_table():
    q = np.arange(BLOCK)[:, None]
    c = np.arange(3 * BLOCK)[None, :]
    rel = c - BLOCK - q
    n = np.abs(rel)
    half = N_BUCKETS // 2
    max_exact = half // 2
    n2 = np.maximum(n, 1).astype(np.int64) ** 2
    floor_log2 = np.floor(np.log2(n2.astype(np.float64)) + 1e-9).astype(np.int64)
    large = np.minimum(max_exact + floor_log2 - 6, half - 1)
    bucket = np.where(rel > 0, half, 0) + np.where(n < max_exact, n, large)
    bucket = np.where(n <= WINDOW, bucket, -1)
    return bucket.astype(np.int32)


BIAS_INTERIOR, BIAS_FIRST, BIAS_LAST = 0, 1, 2


def _bias_kernel(tab_ref, bkt_ref, o_ref):
    h = pl.program_id(0)
    b = bkt_ref[...]
    acc = jnp.full(b.shape, NEG_INF, F32)
    for k in range(N_BUCKETS):
        acc = jnp.where(b == k, tab_ref[k, h] * LOG2E, acc)
    col = lax.broadcasted_iota(jnp.int32, b.shape, 1)
    o_ref[BIAS_INTERIOR, 0] = acc
    o_ref[BIAS_FIRST, 0] = jnp.where(col < BLOCK, NEG_INF, acc)
    o_ref[BIAS_LAST, 0] = jnp.where(col >= 2 * BLOCK, NEG_INF, acc)


def _rel_bias(rel_table):
    bkt = jnp.asarray(_bucket_table())
    return pl.pallas_call(
        _bias_kernel,
        grid=(N_HEADS,),
        in_specs=[
            pl.BlockSpec(memory_space=pltpu.SMEM),
            pl.BlockSpec((BLOCK, 3 * BLOCK), lambda h: (0, 0)),
        ],
        out_specs=pl.BlockSpec((3, 1, BLOCK, 3 * BLOCK), lambda h: (0, h, 0, 0)),
        out_shape=jax.ShapeDtypeStruct((3, N_HEADS, BLOCK, 3 * BLOCK), F32),
        name="relbias",
    )(rel_table, bkt)


def _attn_kernel(sink_ref, q_ref, kvp_ref, kvc_ref, kvn_ref, bias_ref, o_ref, *, nblk, layer):
    i = pl.program_id(1)
    ones = jnp.ones((3 * BLOCK, HEAD_DIM), BF16)
    windows = []
    for kh in range(N_KV_HEADS):
        ks = slice(kh * HEAD_DIM, (kh + 1) * HEAD_DIM)
        vs = slice(KV_WIDTH + kh * HEAD_DIM, KV_WIDTH + (kh + 1) * HEAD_DIM)
        windows.append((jnp.concatenate([kvp_ref[:, ks], kvc_ref[:, ks], kvn_ref[:, ks]], axis=0),
                        jnp.concatenate([kvp_ref[:, vs], kvc_ref[:, vs], kvn_ref[:, vs]], axis=0)))
    for kh, (k_win, v_win) in enumerate(windows):
        for s in range(ATTN_QB):
            n = i * ATTN_QB + s
            variant = jnp.where(n == 0, BIAS_FIRST, jnp.where(n == nblk - 1, BIAS_LAST, BIAS_INTERIOR))
            bias4 = bias_ref[variant, kh * Q_GROUP:(kh + 1) * Q_GROUP].reshape(Q_GROUP * BLOCK, 3 * BLOCK)
            rows = slice(s * BLOCK, (s + 1) * BLOCK)
            group_cols = slice(kh * Q_GROUP * HEAD_DIM, (kh + 1) * Q_GROUP * HEAD_DIM)
            q_blk = q_ref[rows, group_cols]
            q4 = jnp.concatenate([q_blk[:, g * HEAD_DIM:(g + 1) * HEAD_DIM] for g in range(Q_GROUP)], axis=0)
            k_sub = k_win[s * BLOCK:s * BLOCK + 3 * BLOCK]
            v_ext = jnp.concatenate([v_win[s * BLOCK:s * BLOCK + 3 * BLOCK], ones], axis=1)
            sc = lax.dot_general(q4, k_sub, (((1,), (1,)), ((), ())), preferred_element_type=F32) + bias4
            es, sink_terms = [], []
            for g in range(Q_GROUP):
                sg = sc[g * BLOCK:(g + 1) * BLOCK]
                sink2 = sink_ref[layer, kh * Q_GROUP + g] * LOG2E
                m = jnp.maximum(jnp.max(sg, axis=-1, keepdims=True), sink2)
                es.append(jnp.exp2(sg - m).astype(BF16))
                sink_terms.append(jnp.exp2(sink2 - m))
            pv = jnp.dot(jnp.concatenate(es, axis=0), v_ext, preferred_element_type=F32)
            outs = []
            for g in range(Q_GROUP):
                pg = pv[g * BLOCK:(g + 1) * BLOCK]
                outs.append(pg[:, :HEAD_DIM] / (pg[:, HEAD_DIM:] + sink_terms[g]))
            o_ref[rows, group_cols] = jnp.concatenate(outs, axis=1).astype(o_ref.dtype)


def _attention(z3, bias, sink, layer):
    b, s, _ = z3.shape
    nblk = s // BLOCK
    assert nblk >= 2 and nblk % ATTN_QB == 0
    tq = ATTN_QB * BLOCK
    kern = functools.partial(_attn_kernel, nblk=nblk, layer=layer)
    return pl.pallas_call(
        kern,
        grid=(b, s // tq),
        in_specs=[
            pl.BlockSpec(memory_space=pltpu.SMEM),
            pl.BlockSpec((None, tq, ATTN_WIDTH), lambda bi, i: (bi, i, 0)),
            pl.BlockSpec((None, BLOCK, COL), lambda bi, i: (bi, jnp.maximum(i * ATTN_QB - 1, 0), COL_KV)),
            pl.BlockSpec((None, tq, COL), lambda bi, i: (bi, i, COL_KV)),
            pl.BlockSpec((None, BLOCK, COL),
                         lambda bi, i: (bi, jnp.minimum((i + 1) * ATTN_QB, nblk - 1), COL_KV)),
            pl.BlockSpec(bias.shape, lambda bi, i: (0, 0, 0, 0)),
        ],
        out_specs=pl.BlockSpec((None, tq, ATTN_WIDTH), lambda bi, i: (bi, i, 0)),
        out_shape=jax.ShapeDtypeStruct((b, s, ATTN_WIDTH), BF16),
        compiler_params=pltpu.CompilerParams(
            dimension_semantics=("arbitrary", "arbitrary"), vmem_limit_bytes=VMEM_LIMIT),
        name="attn",
    )(sink, z3, z3, z3, z3, bias)


def _segment_permutation(t):
    groups = t // SUBLANES
    i = np.arange(t)
    p = np.zeros((t, t), np.float32)
    p[i, (i % SUBLANES) * groups + i // SUBLANES] = 1.0
    return p


def _lru_gates(xc, log_a_scale, wai_ref, ba_ref, bi_ref, d):
    ra, ri = [], []
    for hd in range(LRU_HEADS):
        xh = xc[:, hd * LRU_BLOCK:(hd + 1) * LRU_BLOCK].astype(BF16)
        y = jnp.dot(xh, wai_ref[d, hd], preferred_element_type=F32)
        ra.append(y[:, :LRU_BLOCK])
        ri.append(y[:, LRU_BLOCK:])
    r = _sigmoid(jnp.concatenate(ra, axis=1) + ba_ref[d:d + 1, :])
    gi = _sigmoid(jnp.concatenate(ri, axis=1) + bi_ref[d:d + 1, :])
    log_a = log_a_scale * r
    a = jnp.exp(log_a)
    v = -jnp.tanh(log_a) * (a * a + 1.0)
    root = jnp.where(v > 0.0, v * lax.rsqrt(v), 0.0)
    return a, root * (gi * xc)


def _lru_direction(x3_scr, halo, carry, a_scr, u_scr, out_ref,
                   lw_ref, lb_ref, wai_ref, ba_ref, bi_ref, lam_ref, d):
    reverse = d == 1
    t = MIX_T
    groups = t // SUBLANES
    w = LRU_WIDTH
    k = LRU_CONV_K - 1
    nl = -lam_ref[d:d + 1, :]
    log_a_scale = -LRU_C * (jnp.maximum(nl, 0.0) + jnp.log1p(jnp.exp(-jnp.abs(nl))))
    x3 = x3_scr[...]
    row = lax.broadcasted_iota(jnp.int32, (1, SUBLANES, w), 1)
    if reverse:
        src = x3[:k]
        wrapped = pltpu.roll(jnp.where(row == 0, halo[...], src), SUBLANES - 1, axis=1)
        ext = jnp.concatenate([x3, wrapped], axis=0)
    else:
        src = x3[groups - k:]
        wrapped = pltpu.roll(jnp.where(row == SUBLANES - 1, halo[...], src), 1, axis=1)
        ext = jnp.concatenate([wrapped, x3], axis=0)
    halo[...] = src
    taps = [lw_ref[d, j:j + 1, :] for j in range(LRU_CONV_K)]
    bias = lb_ref[d:d + 1, :]

    def in_time_order(n):
        return range(n - 1, -1, -1) if reverse else range(n)

    h = p = None
    for ci in in_time_order(groups // MIX_CHUNK):
        g0 = ci * MIX_CHUNK
        xc = bias + sum(taps[j] * ext[g0 + j:g0 + j + MIX_CHUNK] for j in range(LRU_CONV_K))
        a, u = _lru_gates(xc.reshape(MIX_CHUNK * SUBLANES, w), log_a_scale, wai_ref, ba_ref, bi_ref, d)
        a3 = a.reshape(MIX_CHUNK, SUBLANES, w)
        u3 = u.reshape(MIX_CHUNK, SUBLANES, w)
        a_scr[g0:g0 + MIX_CHUNK] = a3
        u_scr[g0:g0 + MIX_CHUNK] = u3
        for g in in_time_order(MIX_CHUNK):
            h = u3[g] if h is None else a3[g] * h + u3[g]
            p = a3[g] if p is None else a3[g] * p

    c = carry[0:1, :]
    cs = [None] * SUBLANES
    for r in in_time_order(SUBLANES):
        cs[r] = c
        c = p[r:r + 1, :] * c + h[r:r + 1, :]
    carry[0:1, :] = c

    h = jnp.concatenate(cs, axis=0)
    for g in in_time_order(groups):
        h = a_scr[g] * h + u_scr[g]
        for j in range(w // LANES):
            out_ref[j, pl.ds(g, SUBLANES, stride=SEG_PITCH), :] = h[:, j * LANES:(j + 1) * LANES]
    for r in range(SUBLANES):
        out_ref[:, r * SEG_PITCH + SEG_ROWS:(r + 1) * SEG_PITCH, :] = jnp.zeros(
            (w // LANES, SEG_PITCH - SEG_ROWS, LANES), F32)


def _mixers_kernel(xf_ref, xb_ref, xfn_ref, xbn_ref, perm_ref, lw_ref, lb_ref, wai_ref, ba_ref, bi_ref, lam_ref,
                   hf_ref, hb_ref, x3_f, x3_b, halo_f, halo_b, carry_f, carry_b, a_f, u_f, a_b, u_b):
    def permuted(x_ref):
        x3 = jnp.dot(perm_ref[...], x_ref[...], preferred_element_type=F32)
        return x3.reshape(MIX_T // SUBLANES, SUBLANES, LRU_WIDTH)

    @pl.when(pl.program_id(1) == 0)
    def _():
        halo_f[...] = jnp.zeros(halo_f.shape, F32)
        halo_b[...] = jnp.zeros(halo_b.shape, F32)
        carry_f[...] = jnp.zeros(carry_f.shape, F32)
        carry_b[...] = jnp.zeros(carry_b.shape, F32)

    @pl.when((pl.program_id(0) == 0) & (pl.program_id(1) == 0))
    def _():
        x3_f[...] = permuted(xf_ref)
        x3_b[...] = permuted(xb_ref)

    lru = (lw_ref, lb_ref, wai_ref, ba_ref, bi_ref, lam_ref)
    _lru_direction(x3_f, halo_f, carry_f, a_f, u_f, hf_ref, *lru, 0)
    _lru_direction(x3_b, halo_b, carry_b, a_b, u_b, hb_ref, *lru, 1)
    x3_f[...] = permuted(xfn_ref)
    x3_b[...] = permuted(xbn_ref)


def _mixers(z3, lru_conv_w, lru_conv_b, wai, b_a, b_i, lam, l):
    b, s, _ = z3.shape
    t = MIX_T
    nt = s // t
    slabs = LRU_WIDTH // LANES
    perm = jnp.asarray(_segment_permutation(t), BF16)
    params = (lru_conv_w, lru_conv_b, wai, b_a, b_i, lam)

    def next_tile(bi, i, reverse):
        step = jnp.minimum(bi * nt + i + 1, b * nt - 1)
        ni = step % nt
        return (step // nt, nt - 1 - ni if reverse else ni, COL_LRU_X)

    return pl.pallas_call(
        _mixers_kernel,
        grid=(b, nt),
        in_specs=[
            pl.BlockSpec((None, t, COL), lambda bi, i: (bi, i, COL_LRU_X)),
            pl.BlockSpec((None, t, COL), lambda bi, i: (bi, nt - 1 - i, COL_LRU_X)),
            pl.BlockSpec((None, t, COL), lambda bi, i: next_tile(bi, i, False)),
            pl.BlockSpec((None, t, COL), lambda bi, i: next_tile(bi, i, True)),
            pl.BlockSpec((t, t), lambda bi, i: (0, 0)),
        ] + [_layer_resident(x, l) for x in params],
        out_specs=[
            pl.BlockSpec((slabs, None, TILE_PITCH, LANES), lambda bi, i: (0, bi * nt + i, 0, 0)),
            pl.BlockSpec((slabs, None, TILE_PITCH, LANES), lambda bi, i: (0, bi * nt + nt - 1 - i, 0, 0)),
        ],
        out_shape=[
            jax.ShapeDtypeStruct((slabs, b * nt, TILE_PITCH, LANES), F32),
            jax.ShapeDtypeStruct((slabs, b * nt, TILE_PITCH, LANES), F32),
        ],
        scratch_shapes=[pltpu.VMEM((t // SUBLANES, SUBLANES, LRU_WIDTH), F32)] * 2 + [
            pltpu.VMEM((LRU_CONV_K - 1, SUBLANES, LRU_WIDTH), F32),
            pltpu.VMEM((LRU_CONV_K - 1, SUBLANES, LRU_WIDTH), F32),
            pltpu.VMEM((SUBLANES, LRU_WIDTH), F32),
            pltpu.VMEM((SUBLANES, LRU_WIDTH), F32),
        ] + [pltpu.VMEM((t // SUBLANES, SUBLANES, LRU_WIDTH), F32)] * 4,
        compiler_params=pltpu.CompilerParams(
            dimension_semantics=("arbitrary", "arbitrary"), vmem_limit_bytes=VMEM_LIMIT),
        name="mixers",
    )(z3, z3, z3, z3, perm, *params)


def _outple_kernel(h_ref, ya_ref, hf_ref, hb_ref, zg_ref, zc_ref, gl_ref, zp_ref, zn_ref, p_ref,
                   cw_ref, wo_ref, pn_ref, wg_ref, wp_ref, fn_ref, o_ref, bufc, *, nt, last):
    i = pl.program_id(0)
    pad = SUBLANES
    tm = OUT_TM
    at_start = i % nt == 0
    at_end = i % nt == nt - 1
    before = (zp_ref[:, :COL].astype(F32) * zp_ref[:, COL:].astype(F32))[pad:]
    after = (zn_ref[:, :COL].astype(F32) * zn_ref[:, COL:].astype(F32))[:pad]
    bufc[0:pad, :] = jnp.where(at_start, 0.0, before)
    bufc[pad + tm:2 * pad + tm, :] = jnp.where(at_end, 0.0, after)
    chunk = MIX_CHUNK * SUBLANES
    for r0 in range(0, tm, chunk):
        rows = slice(r0, r0 + chunk)
        bufc[pad + r0:pad + r0 + chunk, :] = zc_ref[rows, :COL].astype(F32) * zc_ref[rows, COL:2 * COL].astype(F32)

    subs = [slice(s * OUT_SUB, (s + 1) * OUT_SUB) for s in range(OUT_TM // OUT_SUB)]
    h1s, pps = [], []
    for rows in subs:
        g_attn = zg_ref[rows, :ATTN_WIDTH].astype(F32)
        y_attn = (ya_ref[rows, :].astype(F32) * _silu(g_attn)).astype(BF16)
        segs = range(rows.start // SEG_ROWS, rows.stop // SEG_ROWS)
        y_lru = jnp.concatenate(
            [jnp.concatenate([hf_ref[j, r * SEG_PITCH:r * SEG_PITCH + SEG_ROWS, :]
                              + hb_ref[j, r * SEG_PITCH:r * SEG_PITCH + SEG_ROWS, :] for r in segs], axis=0)
             for j in range(LRU_WIDTH // LANES)], axis=1)
        y_lru = (y_lru * _silu(gl_ref[rows, :].astype(F32))).astype(BF16)
        conv = sum(cw_ref[j:j + 1, :] * bufc[pad - 1 + j + rows.start:pad - 1 + j + rows.stop, :]
                   for j in range(CONV_K))
        y_conv = zg_ref[rows, ATTN_WIDTH:].astype(F32) * conv * _silu(zc_ref[rows, 2 * COL:].astype(F32))
        y_conv = y_conv.astype(BF16)
        proj = (jnp.dot(y_attn, wo_ref[0:ATTN_WIDTH, :], preferred_element_type=F32)
                + jnp.dot(y_lru, wo_ref[ATTN_WIDTH + CONV_WIDTH:, :], preferred_element_type=F32)
                + jnp.dot(y_conv, wo_ref[ATTN_WIDTH:ATTN_WIDTH + CONV_WIDTH, :], preferred_element_type=F32))
        h1s.append(h_ref[rows, :] + proj)
    for rows in subs:
        pps.append(jnp.dot(p_ref[rows, :].astype(BF16), wp_ref[...], preferred_element_type=F32))
    for rows, h1, pp in zip(subs, h1s, pps):
        n = _rms(h1, pn_ref[...]).astype(BF16)
        gate = _sigmoid(jnp.dot(n, wg_ref[...], preferred_element_type=F32))
        h2 = h1 + gate * pp
        if last:
            h2 = _rms(h2, fn_ref[...])
        o_ref[rows, :] = h2


def _outple(h2d, ya, hf, hb, z2d, p3d, cw, wo_b, pn, wg_b, wp_b, fn, l, nt, last):
    m = h2d.shape[0]
    tm = OUT_TM
    halo = 16
    kern = functools.partial(_outple_kernel, nt=nt, last=last)

    def rows(width, c=0):
        return pl.BlockSpec((tm, width), lambda i: (i, c))

    assert COL_CONV_B == COL_G_ATTN + 2 and COL_G_ATTN % 3 == 0
    assert (COL_CONV_C, COL_CONV_X, COL_G_CONV) == (6, 7, 8)
    halo_before = pl.BlockSpec((halo, 2 * COL), lambda i: (jnp.maximum(i * (tm // halo) - 1, 0), COL_CONV_C // 2))
    halo_after = pl.BlockSpec(
        (halo, 2 * COL), lambda i: (jnp.minimum((i + 1) * (tm // halo), m // halo - 1), COL_CONV_C // 2))

    assert tm == MIX_T and OUT_SUB % SEG_ROWS == 0
    slab = pl.BlockSpec((LRU_WIDTH // LANES, None, TILE_PITCH, LANES), lambda i: (0, i, 0, 0))
    return pl.pallas_call(
        kern,
        grid=(m // tm,),
        in_specs=[
            rows(D_MODEL), rows(ATTN_WIDTH), slab, slab,
            rows(3 * COL, COL_G_ATTN // 3), rows(3 * COL, COL_CONV_C // 3), rows(COL, COL_G_LRU),
            halo_before, halo_after,
            pl.BlockSpec((None, tm, PLE_DIM), lambda i: (l, i, 0)),
            _layer_resident(cw, l), _layer_resident(wo_b, l), _layer_resident(pn, l), _layer_resident(wg_b, l),
            _layer_resident(wp_b, l), _layer_resident(fn, 0),
        ],
        out_specs=rows(D_MODEL),
        out_shape=jax.ShapeDtypeStruct((m, D_MODEL), F32),
        scratch_shapes=[pltpu.VMEM((tm + 2 * SUBLANES, CONV_WIDTH), F32)],
        compiler_params=pltpu.CompilerParams(
            dimension_semantics=("arbitrary",), vmem_limit_bytes=VMEM_LIMIT),
        name="outple",
    )(h2d, ya, hf, hb, *([z2d] * 5), p3d, cw, wo_b, pn, wg_b, wp_b, fn)


def _trunk(x, p, prm, bias):
    b, s, _ = x.shape
    m = b * s
    depth = p.shape[0]
    h = x.reshape(m, D_MODEL)
    p3d = p.reshape(depth, m, PLE_DIM)
    for l in range(depth):
        z = _inproj(h, prm["norm_mix"], prm["w_in"], l)
        z3 = z.reshape(b, s, IN_WIDTH)
        ya = _attention(z3, bias, prm["attn_sink"], l)
        hf, hb = _mixers(z3, prm["lru_conv_w"], prm["lru_conv_b"],
                         prm["wai"], prm["lru_b_a"], prm["lru_b_i"], prm["lru_L"], l)
        h = _outple(h, ya.reshape(m, ATTN_WIDTH), hf, hb, z, p3d,
                    prm["conv_w"], prm["w_out"], prm["ple_norm"], prm["ple_w_gate"], prm["ple_w_proj"],
                    prm["final_norm"], l, s // OUT_TM, last=(l == depth - 1))
    return h.reshape(b, s, D_MODEL)


def kernel(x_prompt, x_sample, p_prompt, p_sample, norm_mix, w_in, w_out, rel_bias, attn_sink, conv_w,
           lru_conv_w, lru_conv_b, lru_w_a, lru_b_a, lru_w_i, lru_b_i, lru_L, ple_norm, ple_w_gate,
           ple_w_proj, final_norm):
    col_scale = jnp.where(jnp.arange(IN_WIDTH) < ATTN_WIDTH, Q_SCALE, 1.0).astype(F32)
    prm = dict(
        norm_mix=norm_mix[:, None, :], w_in=(w_in * col_scale).astype(BF16), w_out=w_out.astype(BF16),
        attn_sink=attn_sink, conv_w=conv_w, lru_conv_w=lru_conv_w, lru_conv_b=lru_conv_b,
        wai=jnp.concatenate([lru_w_a, lru_w_i], axis=-1).astype(BF16),
        lru_b_a=lru_b_a, lru_b_i=lru_b_i, lru_L=lru_L, ple_norm=ple_norm[:, None, :],
        ple_w_gate=ple_w_gate.astype(BF16), ple_w_proj=ple_w_proj.astype(BF16),
        final_norm=final_norm[None, None, :])
    bias = _rel_bias(rel_bias)
    y_prompt = _trunk(x_prompt, p_prompt, prm, bias)
    y_sample = _trunk(x_sample, p_sample, prm, bias)
    return (y_prompt, y_sample)
```

```python
import functools
import math

import numpy as np
import jax
import jax.numpy as jnp
from jax import lax
from jax.experimental import pallas as pl
from jax.experimental.pallas import tpu as pltpu

F32 = jnp.float32
BF16 = jnp.bfloat16

D_MODEL = 2048
N_HEADS = 8
N_KV_HEADS = 2
Q_GROUP = N_HEADS // N_KV_HEADS
HEAD_DIM = 128
ATTN_WIDTH = N_HEADS * HEAD_DIM
KV_WIDTH = N_KV_HEADS * HEAD_DIM
WINDOW = 128
BLOCK = 128
N_BUCKETS = 32
MAX_DISTANCE = 128
CONV_WIDTH = 512
CONV_K = 3
LRU_WIDTH = 512
LRU_HEADS = 4
LRU_BLOCK = 128
LRU_CONV_K = 4
LRU_C = 8.0
MIX_WIDTH = ATTN_WIDTH + CONV_WIDTH + LRU_WIDTH
IN_WIDTH = 2 * ATTN_WIDTH + 2 * KV_WIDTH + 4 * CONV_WIDTH + 2 * LRU_WIDTH
PLE_DIM = 256
NORM_EPS = 1e-6
NEG_INF = -1e30
LOG2E = math.log2(math.e)
Q_SCALE = HEAD_DIM ** -0.5 * LOG2E

COL = 512
COL_KV = 2
COL_G_ATTN = 3
COL_CONV_B = 5
COL_CONV_C = 6
COL_CONV_X = 7
COL_G_CONV = 8
COL_LRU_X = 9
COL_G_LRU = 10

SUBLANES = 8
LANES = 128
VMEM_LIMIT = 60 * 1024 * 1024

INPROJ_TM = 512
INPROJ_TN = 512
ATTN_QB = 4
MIX_T = 512
MIX_CHUNK = 8
SEG_ROWS = MIX_T // SUBLANES
SEG_PITCH = SEG_ROWS + SUBLANES
TILE_PITCH = SUBLANES * SEG_PITCH
OUT_TM = 512
OUT_SUB = 256


def _sigmoid(x):
    return 1.0 / (1.0 + jnp.exp2(x * (-LOG2E)))


def _silu(x):
    return x * _sigmoid(x)


def _rms(x, g):
    ms = jnp.mean(x * x, axis=-1, keepdims=True)
    return x * lax.rsqrt(ms + NORM_EPS) * g


def _layer_resident(x, l):
    nd = x.ndim - 1
    return pl.BlockSpec((None,) + x.shape[1:], lambda *_: (l,) + (0,) * nd, pipeline_mode=pl.Buffered(1))


def _inproj_kernel(h_ref, g_ref, w_ref, z_ref):
    u = _rms(h_ref[...], g_ref[...]).astype(BF16)
    for c in range(IN_WIDTH // INPROJ_TN):
        cols = slice(c * INPROJ_TN, (c + 1) * INPROJ_TN)
        z_ref[:, cols] = jnp.dot(u, w_ref[:, cols], preferred_element_type=F32).astype(z_ref.dtype)


def _inproj(h2d, g, w_b, l):
    m = h2d.shape[0]
    tm = INPROJ_TM
    return pl.pallas_call(
        _inproj_kernel,
        grid=(m // tm,),
        in_specs=[
            pl.BlockSpec((tm, D_MODEL), lambda i: (i, 0)),
            _layer_resident(g, l),
            _layer_resident(w_b, l),
        ],
        out_specs=pl.BlockSpec((tm, IN_WIDTH), lambda i: (i, 0)),
        out_shape=jax.ShapeDtypeStruct((m, IN_WIDTH), BF16),
        compiler_params=pltpu.CompilerParams(
            dimension_semantics=("arbitrary",), vmem_limit_bytes=VMEM_LIMIT),
        name="inproj",
    )(h2d, g, w_b)


def _bucket_table():
    q = np.arange(BLOCK)[:, None]
    c = np.arange(3 * BLOCK)[None, :]
    rel = c - BLOCK - q
    n = np.abs(rel)
    half = N_BUCKETS // 2
    max_exact = half // 2
    n2 = np.maximum(n, 1).astype(np.int64) ** 2
    floor_log2 = np.floor(np.log2(n2.astype(np.float64)) + 1e-9).astype(np.int64)
    large = np.minimum(max_exact + floor_log2 - 6, half - 1)
    bucket = np.where(rel > 0, half, 0) + np.where(n < max_exact, n, large)
    bucket = np.where(n <= WINDOW, bucket, -1)
    return bucket.astype(np.int32)


BIAS_INTERIOR, BIAS_FIRST, BIAS_LAST = 0, 1, 2


def _bias_kernel(tab_ref, bkt_ref, o_ref):
    h = pl.program_id(0)
    b = bkt_ref[...]
    acc = jnp.full(b.shape, NEG_INF, F32)
    for k in range(N_BUCKETS):
        acc = jnp.where(b == k, tab_ref[k, h] * LOG2E, acc)
    col = lax.broadcasted_iota(jnp.int32, b.shape, 1)
    o_ref[BIAS_INTERIOR, 0] = acc
    o_ref[BIAS_FIRST, 0] = jnp.where(col < BLOCK, NEG_INF, acc)
    o_ref[BIAS_LAST, 0] = jnp.where(col >= 2 * BLOCK, NEG_INF, acc)


def _rel_bias(rel_table):
    bkt = jnp.asarray(_bucket_table())
    return pl.pallas_call(
        _bias_kernel,
        grid=(N_HEADS,),
        in_specs=[
            pl.BlockSpec(memory_space=pltpu.SMEM),
            pl.BlockSpec((BLOCK, 3 * BLOCK), lambda h: (0, 0)),
        ],
        out_specs=pl.BlockSpec((3, 1, BLOCK, 3 * BLOCK), lambda h: (0, h, 0, 0)),
        out_shape=jax.ShapeDtypeStruct((3, N_HEADS, BLOCK, 3 * BLOCK), F32),
        name="relbias",
    )(rel_table, bkt)


def _attn_kernel(sink_ref, q_ref, kvp_ref, kvc_ref, kvn_ref, bias_ref, o_ref, *, nblk, layer):
    i = pl.program_id(1)
    ones = jnp.ones((3 * BLOCK, HEAD_DIM), BF16)
    windows = []
    for kh in range(N_KV_HEADS):
        ks = slice(kh * HEAD_DIM, (kh + 1) * HEAD_DIM)
        vs = slice(KV_WIDTH + kh * HEAD_DIM, KV_WIDTH + (kh + 1) * HEAD_DIM)
        windows.append((jnp.concatenate([kvp_ref[:, ks], kvc_ref[:, ks], kvn_ref[:, ks]], axis=0),
                        jnp.concatenate([kvp_ref[:, vs], kvc_ref[:, vs], kvn_ref[:, vs]], axis=0)))
    for kh, (k_win, v_win) in enumerate(windows):
        for s in range(ATTN_QB):
            n = i * ATTN_QB + s
            variant = jnp.where(n == 0, BIAS_FIRST, jnp.where(n == nblk - 1, BIAS_LAST, BIAS_INTERIOR))
            bias4 = bias_ref[variant, kh * Q_GROUP:(kh + 1) * Q_GROUP].reshape(Q_GROUP * BLOCK, 3 * BLOCK)
            rows = slice(s * BLOCK, (s + 1) * BLOCK)
            group_cols = slice(kh * Q_GROUP * HEAD_DIM, (kh + 1) * Q_GROUP * HEAD_DIM)
            q_blk = q_ref[rows, group_cols]
            q4 = jnp.concatenate([q_blk[:, g * HEAD_DIM:(g + 1) * HEAD_DIM] for g in range(Q_GROUP)], axis=0)
            k_sub = k_win[s * BLOCK:s * BLOCK + 3 * BLOCK]
            v_ext = jnp.concatenate([v_win[s * BLOCK:s * BLOCK + 3 * BLOCK], ones], axis=1)
            sc = lax.dot_general(q4, k_sub, (((1,), (1,)), ((), ())), preferred_element_type=F32) + bias4
            es, sink_terms = [], []
            for g in range(Q_GROUP):
                sg = sc[g * BLOCK:(g + 1) * BLOCK]
                sink2 = sink_ref[layer, kh * Q_GROUP + g] * LOG2E
                m = jnp.maximum(jnp.max(sg, axis=-1, keepdims=True), sink2)
                es.append(jnp.exp2(sg - m).astype(BF16))
                sink_terms.append(jnp.exp2(sink2 - m))
            pv = jnp.dot(jnp.concatenate(es, axis=0), v_ext, preferred_element_type=F32)
            outs = []
            for g in range(Q_GROUP):
                pg = pv[g * BLOCK:(g + 1) * BLOCK]
                outs.append(pg[:, :HEAD_DIM] / (pg[:, HEAD_DIM:] + sink_terms[g]))
            o_ref[rows, group_cols] = jnp.concatenate(outs, axis=1).astype(o_ref.dtype)
            yield


def _segment_permutation(t):
    groups = t // SUBLANES
    i = np.arange(t)
    p = np.zeros((t, t), np.float32)
    p[i, (i % SUBLANES) * groups + i // SUBLANES] = 1.0
    return p


def _lru_gates(xc, log_a_scale, wai_ref, ba_ref, bi_ref, d):
    ra, ri = [], []
    for hd in range(LRU_HEADS):
        xh = xc[:, hd * LRU_BLOCK:(hd + 1) * LRU_BLOCK].astype(BF16)
        y = jnp.dot(xh, wai_ref[d, hd], preferred_element_type=F32)
        ra.append(y[:, :LRU_BLOCK])
        ri.append(y[:, LRU_BLOCK:])
    r = _sigmoid(jnp.concatenate(ra, axis=1) + ba_ref[d:d + 1, :])
    gi = _sigmoid(jnp.concatenate(ri, axis=1) + bi_ref[d:d + 1, :])
    log_a = log_a_scale * r
    a = jnp.exp(log_a)
    v = -jnp.tanh(log_a) * (a * a + 1.0)
    root = jnp.where(v > 0.0, v * lax.rsqrt(v), 0.0)
    return a, root * (gi * xc)


def _lru_direction(x3_scr, halo, carry, a_scr, u_scr, out_ref,
                   lw_ref, lb_ref, wai_ref, ba_ref, bi_ref, lam_ref, d):
    reverse = d == 1
    t = MIX_T
    groups = t // SUBLANES
    w = LRU_WIDTH
    k = LRU_CONV_K - 1
    nl = -lam_ref[d:d + 1, :]
    log_a_scale = -LRU_C * (jnp.maximum(nl, 0.0) + jnp.log1p(jnp.exp(-jnp.abs(nl))))
    x3 = x3_scr[...]
    row = lax.broadcasted_iota(jnp.int32, (1, SUBLANES, w), 1)
    if reverse:
        src = x3[:k]
        wrapped = pltpu.roll(jnp.where(row == 0, halo[...], src), SUBLANES - 1, axis=1)
        ext = jnp.concatenate([x3, wrapped], axis=0)
    else:
        src = x3[groups - k:]
        wrapped = pltpu.roll(jnp.where(row == SUBLANES - 1, halo[...], src), 1, axis=1)
        ext = jnp.concatenate([wrapped, x3], axis=0)
    halo[...] = src
    taps = [lw_ref[d, j:j + 1, :] for j in range(LRU_CONV_K)]
    bias = lb_ref[d:d + 1, :]

    def in_time_order(n):
        return range(n - 1, -1, -1) if reverse else range(n)

    h = p = None
    for ci in in_time_order(groups // MIX_CHUNK):
        g0 = ci * MIX_CHUNK
        xc = bias + sum(taps[j] * ext[g0 + j:g0 + j + MIX_CHUNK] for j in range(LRU_CONV_K))
        a, u = _lru_gates(xc.reshape(MIX_CHUNK * SUBLANES, w), log_a_scale, wai_ref, ba_ref, bi_ref, d)
        a3 = a.reshape(MIX_CHUNK, SUBLANES, w)
        u3 = u.reshape(MIX_CHUNK, SUBLANES, w)
        a_scr[g0:g0 + MIX_CHUNK] = a3
        u_scr[g0:g0 + MIX_CHUNK] = u3
        for g in in_time_order(MIX_CHUNK):
            h = u3[g] if h is None else a3[g] * h + u3[g]
            p = a3[g] if p is None else a3[g] * p
        yield

    c = carry[0:1, :]
    cs = [None] * SUBLANES
    for r in in_time_order(SUBLANES):
        cs[r] = c
        c = p[r:r + 1, :] * c + h[r:r + 1, :]
    carry[0:1, :] = c

    h = jnp.concatenate(cs, axis=0)
    for n, g in enumerate(in_time_order(groups)):
        h = a_scr[g] * h + u_scr[g]
        for j in range(w // LANES):
            out_ref[j, pl.ds(g, SUBLANES, stride=SEG_PITCH), :] = h[:, j * LANES:(j + 1) * LANES]
        if n % MIX_CHUNK == MIX_CHUNK - 1:
            yield
    for r in range(SUBLANES):
        out_ref[:, r * SEG_PITCH + SEG_ROWS:(r + 1) * SEG_PITCH, :] = jnp.zeros(
            (w // LANES, SEG_PITCH - SEG_ROWS, LANES), F32)


def _round_robin(*gens):
    live = list(gens)
    while live:
        live = [g for g in live if next(g, StopIteration) is not StopIteration]


N_ATTN_IN = 6
N_LRU_IN = 11


def _mid_kernel(*refs, nblk, layer):
    attn_in = refs[:N_ATTN_IN]
    xf_ref, xb_ref, xfn_ref, xbn_ref, perm_ref = refs[N_ATTN_IN:N_ATTN_IN + 5]
    lru = refs[N_ATTN_IN + 5:N_ATTN_IN + N_LRU_IN]
    ya_ref, hf_ref, hb_ref = refs[N_ATTN_IN + N_LRU_IN:N_ATTN_IN + N_LRU_IN + 3]
    x3_f, x3_b, halo_f, halo_b, carry_f, carry_b, a_f, u_f, a_b, u_b = refs[N_ATTN_IN + N_LRU_IN + 3:]

    def permuted(x_ref):
        x3 = jnp.dot(perm_ref[...], x_ref[...], preferred_element_type=F32)
        return x3.reshape(MIX_T // SUBLANES, SUBLANES, LRU_WIDTH)

    @pl.when(pl.program_id(1) == 0)
    def _():
        halo_f[...] = jnp.zeros(halo_f.shape, F32)
        halo_b[...] = jnp.zeros(halo_b.shape, F32)
        carry_f[...] = jnp.zeros(carry_f.shape, F32)
        carry_b[...] = jnp.zeros(carry_b.shape, F32)

    @pl.when((pl.program_id(0) == 0) & (pl.program_id(1) == 0))
    def _():
        x3_f[...] = permuted(xf_ref)
        x3_b[...] = permuted(xb_ref)

    _round_robin(
        _attn_kernel(*attn_in, ya_ref, nblk=nblk, layer=layer),
        _lru_direction(x3_f, halo_f, carry_f, a_f, u_f, hf_ref, *lru, 0),
        _lru_direction(x3_b, halo_b, carry_b, a_b, u_b, hb_ref, *lru, 1))
    x3_f[...] = permuted(xfn_ref)
    x3_b[...] = permuted(xbn_ref)


def _mid(z3, bias, sink, lru_conv_w, lru_conv_b, wai, b_a, b_i, lam, l):
    b, s, _ = z3.shape
    t = MIX_T
    nt = s // t
    nblk = s // BLOCK
    assert nblk >= 2 and t == ATTN_QB * BLOCK
    slabs = LRU_WIDTH // LANES
    perm = jnp.asarray(_segment_permutation(t), BF16)
    params = (lru_conv_w, lru_conv_b, wai, b_a, b_i, lam)

    def next_tile(bi, i, reverse):
        step = jnp.minimum(bi * nt + i + 1, b * nt - 1)
        ni = step % nt
        return (step // nt, nt - 1 - ni if reverse else ni, COL_LRU_X)

    attn_specs = [
        pl.BlockSpec(memory_space=pltpu.SMEM),
        pl.BlockSpec((None, t, ATTN_WIDTH), lambda bi, i: (bi, i, 0)),
        pl.BlockSpec((None, BLOCK, COL), lambda bi, i: (bi, jnp.maximum(i * ATTN_QB - 1, 0), COL_KV)),
        pl.BlockSpec((None, t, COL), lambda bi, i: (bi, i, COL_KV)),
        pl.BlockSpec((None, BLOCK, COL), lambda bi, i: (bi, jnp.minimum((i + 1) * ATTN_QB, nblk - 1), COL_KV)),
        pl.BlockSpec(bias.shape, lambda bi, i: (0, 0, 0, 0)),
    ]
    lru_specs = [
        pl.BlockSpec((None, t, COL), lambda bi, i: (bi, i, COL_LRU_X)),
        pl.BlockSpec((None, t, COL), lambda bi, i: (bi, nt - 1 - i, COL_LRU_X)),
        pl.BlockSpec((None, t, COL), lambda bi, i: next_tile(bi, i, False)),
        pl.BlockSpec((None, t, COL), lambda bi, i: next_tile(bi, i, True)),
        pl.BlockSpec((t, t), lambda bi, i: (0, 0)),
    ] + [_layer_resident(x, l) for x in params]
    assert len(attn_specs) == N_ATTN_IN and len(lru_specs) == N_LRU_IN
    return pl.pallas_call(
        functools.partial(_mid_kernel, nblk=nblk, layer=l),
        grid=(b, nt),
        in_specs=attn_specs + lru_specs,
        out_specs=[
            pl.BlockSpec((None, t, ATTN_WIDTH), lambda bi, i: (bi, i, 0)),
            pl.BlockSpec((slabs, None, TILE_PITCH, LANES), lambda bi, i: (0, bi * nt + i, 0, 0)),
            pl.BlockSpec((slabs, None, TILE_PITCH, LANES), lambda bi, i: (0, bi * nt + nt - 1 - i, 0, 0)),
        ],
        out_shape=[
            jax.ShapeDtypeStruct((b, s, ATTN_WIDTH), BF16),
            jax.ShapeDtypeStruct((slabs, b * nt, TILE_PITCH, LANES), F32),
            jax.ShapeDtypeStruct((slabs, b * nt, TILE_PITCH, LANES), F32),
        ],
        scratch_shapes=[pltpu.VMEM((t // SUBLANES, SUBLANES, LRU_WIDTH), F32)] * 2 + [
            pltpu.VMEM((LRU_CONV_K - 1, SUBLANES, LRU_WIDTH), F32),
            pltpu.VMEM((LRU_CONV_K - 1, SUBLANES, LRU_WIDTH), F32),
            pltpu.VMEM((SUBLANES, LRU_WIDTH), F32),
            pltpu.VMEM((SUBLANES, LRU_WIDTH), F32),
        ] + [pltpu.VMEM((t // SUBLANES, SUBLANES, LRU_WIDTH), F32)] * 4,
        compiler_params=pltpu.CompilerParams(
            dimension_semantics=("arbitrary", "arbitrary"), vmem_limit_bytes=VMEM_LIMIT),
        name="mid",
    )(sink, z3, z3, z3, z3, bias, z3, z3, z3, z3, perm, *params)


def _outple_kernel(h_ref, ya_ref, hf_ref, hb_ref, zg_ref, zc_ref, gl_ref, zp_ref, zn_ref, p_ref,
                   cw_ref, wo_ref, pn_ref, wg_ref, wp_ref, fn_ref, o_ref, bufc, *, nt, last):
    i = pl.program_id(0)
    pad = SUBLANES
    tm = OUT_TM
    at_start = i % nt == 0
    at_end = i % nt == nt - 1
    before = (zp_ref[:, :COL].astype(F32) * zp_ref[:, COL:].astype(F32))[pad:]
    after = (zn_ref[:, :COL].astype(F32) * zn_ref[:, COL:].astype(F32))[:pad]
    bufc[0:pad, :] = jnp.where(at_start, 0.0, before)
    bufc[pad + tm:2 * pad + tm, :] = jnp.where(at_end, 0.0, after)
    chunk = MIX_CHUNK * SUBLANES
    for r0 in range(0, tm, chunk):
        rows = slice(r0, r0 + chunk)
        bufc[pad + r0:pad + r0 + chunk, :] = zc_ref[rows, :COL].astype(F32) * zc_ref[rows, COL:2 * COL].astype(F32)

    subs = [slice(s * OUT_SUB, (s + 1) * OUT_SUB) for s in range(OUT_TM // OUT_SUB)]
    h1s, pps = [], []
    for rows in subs:
        g_attn = zg_ref[rows, :ATTN_WIDTH].astype(F32)
        y_attn = (ya_ref[rows, :].astype(F32) * _silu(g_attn)).astype(BF16)
        segs = range(rows.start // SEG_ROWS, rows.stop // SEG_ROWS)
        y_lru = jnp.concatenate(
            [jnp.concatenate([hf_ref[j, r * SEG_PITCH:r * SEG_PITCH + SEG_ROWS, :]
                              + hb_ref[j, r * SEG_PITCH:r * SEG_PITCH + SEG_ROWS, :] for r in segs], axis=0)
             for j in range(LRU_WIDTH // LANES)], axis=1)
        y_lru = (y_lru * _silu(gl_ref[rows, :].astype(F32))).astype(BF16)
        conv = sum(cw_ref[j:j + 1, :] * bufc[pad - 1 + j + rows.start:pad - 1 + j + rows.stop, :]
                   for j in range(CONV_K))
        y_conv = zg_ref[rows, ATTN_WIDTH:].astype(F32) * conv * _silu(zc_ref[rows, 2 * COL:].astype(F32))
        y_conv = y_conv.astype(BF16)
        proj = (jnp.dot(y_attn, wo_ref[0:ATTN_WIDTH, :], preferred_element_type=F32)
                + jnp.dot(y_lru, wo_ref[ATTN_WIDTH + CONV_WIDTH:, :], preferred_element_type=F32)
                + jnp.dot(y_conv, wo_ref[ATTN_WIDTH:ATTN_WIDTH + CONV_WIDTH, :], preferred_element_type=F32))
        h1s.append(h_ref[rows, :] + proj)
    for rows in subs:
        pps.append(jnp.dot(p_ref[rows, :].astype(BF16), wp_ref[...], preferred_element_type=F32))
    for rows, h1, pp in zip(subs, h1s, pps):
        n = _rms(h1, pn_ref[...]).astype(BF16)
        gate = _sigmoid(jnp.dot(n, wg_ref[...], preferred_element_type=F32))
        h2 = h1 + gate * pp
        if last:
            h2 = _rms(h2, fn_ref[...])
        o_ref[rows, :] = h2


def _outple(h2d, ya, hf, hb, z2d, p3d, cw, wo_b, pn, wg_b, wp_b, fn, l, nt, last):
    m = h2d.shape[0]
    tm = OUT_TM
    halo = 16
    kern = functools.partial(_outple_kernel, nt=nt, last=last)

    def rows(width, c=0):
        return pl.BlockSpec((tm, width), lambda i: (i, c))

    assert COL_CONV_B == COL_G_ATTN + 2 and COL_G_ATTN % 3 == 0
    assert (COL_CONV_C, COL_CONV_X, COL_G_CONV) == (6, 7, 8)
    halo_before = pl.BlockSpec((halo, 2 * COL), lambda i: (jnp.maximum(i * (tm // halo) - 1, 0), COL_CONV_C // 2))
    halo_after = pl.BlockSpec(
        (halo, 2 * COL), lambda i: (jnp.minimum((i + 1) * (tm // halo), m // halo - 1), COL_CONV_C // 2))

    assert tm == MIX_T and OUT_SUB % SEG_ROWS == 0
    slab = pl.BlockSpec((LRU_WIDTH // LANES, None, TILE_PITCH, LANES), lambda i: (0, i, 0, 0))
    return pl.pallas_call(
        kern,
        grid=(m // tm,),
        in_specs=[
            rows(D_MODEL), rows(ATTN_WIDTH), slab, slab,
            rows(3 * COL, COL_G_ATTN // 3), rows(3 * COL, COL_CONV_C // 3), rows(COL, COL_G_LRU),
            halo_before, halo_after,
            pl.BlockSpec((None, tm, PLE_DIM), lambda i: (l, i, 0)),
            _layer_resident(cw, l), _layer_resident(wo_b, l), _layer_resident(pn, l), _layer_resident(wg_b, l),
            _layer_resident(wp_b, l), _layer_resident(fn, 0),
        ],
        out_specs=rows(D_MODEL),
        out_shape=jax.ShapeDtypeStruct((m, D_MODEL), F32),
        scratch_shapes=[pltpu.VMEM((tm + 2 * SUBLANES, CONV_WIDTH), F32)],
        compiler_params=pltpu.CompilerParams(
            dimension_semantics=("arbitrary",), vmem_limit_bytes=VMEM_LIMIT),
        name="outple",
    )(h2d, ya, hf, hb, *([z2d] * 5), p3d, cw, wo_b, pn, wg_b, wp_b, fn)


def _trunk(x, p, prm, bias):
    b, s, _ = x.shape
    m = b * s
    depth = p.shape[0]
    h = x.reshape(m, D_MODEL)
    p3d = p.reshape(depth, m, PLE_DIM)
    for l in range(depth):
        z = _inproj(h, prm["norm_mix"], prm["w_in"], l)
        z3 = z.reshape(b, s, IN_WIDTH)
        ya, hf, hb = _mid(z3, bias, prm["attn_sink"], prm["lru_conv_w"], prm["lru_conv_b"],
                          prm["wai"], prm["lru_b_a"], prm["lru_b_i"], prm["lru_L"], l)
        h = _outple(h, ya.reshape(m, ATTN_WIDTH), hf, hb, z, p3d,
                    prm["conv_w"], prm["w_out"], prm["ple_norm"], prm["ple_w_gate"], prm["ple_w_proj"],
                    prm["final_norm"], l, s // OUT_TM, last=(l == depth - 1))
    return h.reshape(b, s, D_MODEL)


def kernel(x_prompt, x_sample, p_prompt, p_sample, norm_mix, w_in, w_out, rel_bias, attn_sink, conv_w,
           lru_conv_w, lru_conv_b, lru_w_a, lru_b_a, lru_w_i, lru_b_i, lru_L, ple_norm, ple_w_gate,
           ple_w_proj, final_norm):
    col_scale = jnp.where(jnp.arange(IN_WIDTH) < ATTN_WIDTH, Q_SCALE, 1.0).astype(F32)
    prm = dict(
        norm_mix=norm_mix[:, None, :], w_in=(w_in * col_scale).astype(BF16), w_out=w_out.astype(BF16),
        attn_sink=attn_sink, conv_w=conv_w, lru_conv_w=lru_conv_w, lru_conv_b=lru_conv_b,
        wai=jnp.concatenate([lru_w_a, lru_w_i], axis=-1).astype(BF16),
        lru_b_a=lru_b_a, lru_b_i=lru_b_i, lru_L=lru_L, ple_norm=ple_norm[:, None, :],
        ple_w_gate=ple_w_gate.astype(BF16), ple_w_proj=ple_w_proj.astype(BF16),
        final_norm=final_norm[None, None, :])
    bias = _rel_bias(rel_bias)
    y_prompt = _trunk(x_prompt, p_prompt, prm, bias)
    y_sample = _trunk(x_sample, p_sample, prm, bias)
    return (y_prompt, y_sample)
```

```python
import functools
import math

import numpy as np
import jax
import jax.numpy as jnp
from jax import lax
from jax.experimental import pallas as pl
from jax.experimental.pallas import tpu as pltpu

F32 = jnp.float32
BF16 = jnp.bfloat16

D_MODEL = 2048
N_HEADS = 8
N_KV_HEADS = 2
Q_GROUP = N_HEADS // N_KV_HEADS
HEAD_DIM = 128
ATTN_WIDTH = N_HEADS * HEAD_DIM
KV_WIDTH = N_KV_HEADS * HEAD_DIM
WINDOW = 128
BLOCK = 128
N_BUCKETS = 32
MAX_DISTANCE = 128
CONV_WIDTH = 512
CONV_K = 3
LRU_WIDTH = 512
LRU_HEADS = 4
LRU_BLOCK = 128
LRU_CONV_K = 4
LRU_C = 8.0
MIX_WIDTH = ATTN_WIDTH + CONV_WIDTH + LRU_WIDTH
IN_WIDTH = 2 * ATTN_WIDTH + 2 * KV_WIDTH + 4 * CONV_WIDTH + 2 * LRU_WIDTH
PLE_DIM = 256
NORM_EPS = 1e-6
NEG_INF = -1e30
LOG2E = math.log2(math.e)
Q_SCALE = HEAD_DIM ** -0.5 * LOG2E

COL = 512
COL_KV = 2
COL_G_ATTN = 3
COL_CONV_B = 5
COL_CONV_C = 6
COL_CONV_X = 7
COL_G_CONV = 8
COL_LRU_X = 9
COL_G_LRU = 10

SUBLANES = 8
LANES = 128
VMEM_LIMIT = 60 * 1024 * 1024

INPROJ_TM = 512
INPROJ_TN = 512
ATTN_QB = 4
MIX_T = 512
MIX_CHUNK = 16
SEG_ROWS = MIX_T // SUBLANES
SEG_PITCH = SEG_ROWS + SUBLANES // 2
TILE_PITCH = SUBLANES * SEG_PITCH
OUT_TM = 512
OUT_SUB = 256


def _sigmoid(x):
    return 1.0 / (1.0 + jnp.exp2(x * (-LOG2E)))


def _silu(x):
    return x * _sigmoid(x)


def _rms(x, g):
    ms = jnp.mean(x * x, axis=-1, keepdims=True)
    return x * lax.rsqrt(ms + NORM_EPS) * g


def _layer_resident(x, l):
    nd = x.ndim - 1
    return pl.BlockSpec((None,) + x.shape[1:], lambda *_: (l,) + (0,) * nd, pipeline_mode=pl.Buffered(1))


def _inproj_kernel(h_ref, g_ref, w_ref, z_ref):
    u = _rms(h_ref[...], g_ref[...]).astype(BF16)
    for c in range(IN_WIDTH // INPROJ_TN):
        cols = slice(c * INPROJ_TN, (c + 1) * INPROJ_TN)
        z_ref[:, cols] = jnp.dot(u, w_ref[:, cols], preferred_element_type=F32).astype(z_ref.dtype)


def _inproj(h2d, g, w_b, l):
    m = h2d.shape[0]
    tm = INPROJ_TM
    return pl.pallas_call(
        _inproj_kernel,
        grid=(m // tm,),
        in_specs=[
            pl.BlockSpec((tm, D_MODEL), lambda i: (i, 0)),
            _layer_resident(g, l),
            _layer_resident(w_b, l),
        ],
        out_specs=pl.BlockSpec((tm, IN_WIDTH), lambda i: (i, 0)),
        out_shape=jax.ShapeDtypeStruct((m, IN_WIDTH), BF16),
        compiler_params=pltpu.CompilerParams(
            dimension_semantics=("arbitrary",), vmem_limit_bytes=VMEM_LIMIT),
        name="inproj",
    )(h2d, g, w_b)


def _bucket_table():
    q = np.arange(BLOCK)[:, None]
    c = np.arange(3 * BLOCK)[None, :]
    rel = c - BLOCK - q
    n = np.abs(rel)
    half = N_BUCKETS // 2
    max_exact = half // 2
    n2 = np.maximum(n, 1).astype(np.int64) ** 2
    floor_log2 = np.floor(np.log2(n2.astype(np.float64)) + 1e-9).astype(np.int64)
    large = np.minimum(max_exact + floor_log2 - 6, half - 1)
    bucket = np.where(rel > 0, half, 0) + np.where(n < max_exact, n, large)
    bucket = np.where(n <= WINDOW, bucket, -1)
    return bucket.astype(np.int32)


BIAS_INTERIOR, BIAS_FIRST, BIAS_LAST = 0, 1, 2


def _bias_kernel(tab_ref, bkt_ref, o_ref):
    h = pl.program_id(0)
    b = bkt_ref[...]
    acc = jnp.full(b.shape, NEG_INF, F32)
    for k in range(N_BUCKETS):
        acc = jnp.where(b == k, tab_ref[k, h] * LOG2E, acc)
    col = lax.broadcasted_iota(jnp.int32, b.shape, 1)
    o_ref[BIAS_INTERIOR, 0] = acc
    o_ref[BIAS_FIRST, 0] = jnp.where(col < BLOCK, NEG_INF, acc)
    o_ref[BIAS_LAST, 0] = jnp.where(col >= 2 * BLOCK, NEG_INF, acc)


def _rel_bias(rel_table):
    bkt = jnp.asarray(_bucket_table())
    return pl.pallas_call(
        _bias_kernel,
        grid=(N_HEADS,),
        in_specs=[
            pl.BlockSpec(memory_space=pltpu.SMEM),
            pl.BlockSpec((BLOCK, 3 * BLOCK), lambda h: (0, 0)),
        ],
        out_specs=pl.BlockSpec((3, 1, BLOCK, 3 * BLOCK), lambda h: (0, h, 0, 0)),
        out_shape=jax.ShapeDtypeStruct((3, N_HEADS, BLOCK, 3 * BLOCK), F32),
        name="relbias",
    )(rel_table, bkt)


def _attn_kernel(sink_ref, q_ref, kvp_ref, kvc_ref, kvn_ref, bias_ref, o_ref, *, nblk, layer):
    i = pl.program_id(1)
    ones = jnp.ones((3 * BLOCK, HEAD_DIM), BF16)
    windows = []
    for kh in range(N_KV_HEADS):
        ks = slice(kh * HEAD_DIM, (kh + 1) * HEAD_DIM)
        vs = slice(KV_WIDTH + kh * HEAD_DIM, KV_WIDTH + (kh + 1) * HEAD_DIM)
        windows.append((jnp.concatenate([kvp_ref[:, ks], kvc_ref[:, ks], kvn_ref[:, ks]], axis=0),
                        jnp.concatenate([kvp_ref[:, vs], kvc_ref[:, vs], kvn_ref[:, vs]], axis=0)))
    for kh, (k_win, v_win) in enumerate(windows):
        for s in range(ATTN_QB):
            n = i * ATTN_QB + s
            variant = jnp.where(n == 0, BIAS_FIRST, jnp.where(n == nblk - 1, BIAS_LAST, BIAS_INTERIOR))
            bias4 = bias_ref[variant, kh * Q_GROUP:(kh + 1) * Q_GROUP].reshape(Q_GROUP * BLOCK, 3 * BLOCK)
            rows = slice(s * BLOCK, (s + 1) * BLOCK)
            group_cols = slice(kh * Q_GROUP * HEAD_DIM, (kh + 1) * Q_GROUP * HEAD_DIM)
            q_blk = q_ref[rows, group_cols]
            q4 = jnp.concatenate([q_blk[:, g * HEAD_DIM:(g + 1) * HEAD_DIM] for g in range(Q_GROUP)], axis=0)
            k_sub = k_win[s * BLOCK:s * BLOCK + 3 * BLOCK]
            v_ext = jnp.concatenate([v_win[s * BLOCK:s * BLOCK + 3 * BLOCK], ones], axis=1)
            sc = lax.dot_general(q4, k_sub, (((1,), (1,)), ((), ())), preferred_element_type=F32) + bias4
            es, sink_terms = [], []
            for g in range(Q_GROUP):
                sg = sc[g * BLOCK:(g + 1) * BLOCK]
                sink2 = sink_ref[layer, kh * Q_GROUP + g] * LOG2E
                m = jnp.maximum(jnp.max(sg, axis=-1, keepdims=True), sink2)
                es.append(jnp.exp2(sg - m).astype(BF16))
                sink_terms.append(jnp.exp2(sink2 - m))
            pv = jnp.dot(jnp.concatenate(es, axis=0), v_ext, preferred_element_type=F32)
            outs = []
            for g in range(Q_GROUP):
                pg = pv[g * BLOCK:(g + 1) * BLOCK]
                outs.append(pg[:, :HEAD_DIM] / (pg[:, HEAD_DIM:] + sink_terms[g]))
            o_ref[rows, group_cols] = jnp.concatenate(outs, axis=1).astype(o_ref.dtype)


def _attention(z3, bias, sink, layer):
    b, s, _ = z3.shape
    nblk = s // BLOCK
    assert nblk >= 2 and nblk % ATTN_QB == 0
    tq = ATTN_QB * BLOCK
    kern = functools.partial(_attn_kernel, nblk=nblk, layer=layer)
    return pl.pallas_call(
        kern,
        grid=(b, s // tq),
        in_specs=[
            pl.BlockSpec(memory_space=pltpu.SMEM),
            pl.BlockSpec((None, tq, ATTN_WIDTH), lambda bi, i: (bi, i, 0)),
            pl.BlockSpec((None, BLOCK, COL), lambda bi, i: (bi, jnp.maximum(i * ATTN_QB - 1, 0), COL_KV)),
            pl.BlockSpec((None, tq, COL), lambda bi, i: (bi, i, COL_KV)),
            pl.BlockSpec((None, BLOCK, COL),
                         lambda bi, i: (bi, jnp.minimum((i + 1) * ATTN_QB, nblk - 1), COL_KV)),
            pl.BlockSpec(bias.shape, lambda bi, i: (0, 0, 0, 0)),
        ],
        out_specs=pl.BlockSpec((None, tq, ATTN_WIDTH), lambda bi, i: (bi, i, 0)),
        out_shape=jax.ShapeDtypeStruct((b, s, ATTN_WIDTH), BF16),
        compiler_params=pltpu.CompilerParams(
            dimension_semantics=("arbitrary", "arbitrary"), vmem_limit_bytes=VMEM_LIMIT),
        name="attn",
    )(sink, z3, z3, z3, z3, bias)


def _segment_permutation(t):
    groups = t // SUBLANES
    i = np.arange(t)
    p = np.zeros((t, t), np.float32)
    p[i, (i % SUBLANES) * groups + i // SUBLANES] = 1.0
    return p


def _lru_gates(xc, log_a_scale, wai_ref, ba_ref, bi_ref, d):
    ra, ri = [], []
    for hd in range(LRU_HEADS):
        xh = xc[:, hd * LRU_BLOCK:(hd + 1) * LRU_BLOCK].astype(BF16)
        y = jnp.dot(xh, wai_ref[d, hd], preferred_element_type=F32)
        ra.append(y[:, :LRU_BLOCK])
        ri.append(y[:, LRU_BLOCK:])
    r = _sigmoid(jnp.concatenate(ra, axis=1) + ba_ref[d:d + 1, :])
    gi = _sigmoid(jnp.concatenate(ri, axis=1) + bi_ref[d:d + 1, :])
    log_a = log_a_scale * r
    a = jnp.exp(log_a)
    v = -jnp.tanh(log_a) * (a * a + 1.0)
    root = jnp.where(v > 0.0, v * lax.rsqrt(v), 0.0)
    return a, root * (gi * xc)


def _lru_direction(x3_scr, halo, carry, a_scr, u_scr, out_ref,
                   lw_ref, lb_ref, wai_ref, ba_ref, bi_ref, lam_ref, d):
    reverse = d == 1
    t = MIX_T
    groups = t // SUBLANES
    w = LRU_WIDTH
    k = LRU_CONV_K - 1
    nl = -lam_ref[d:d + 1, :]
    log_a_scale = -LRU_C * (jnp.maximum(nl, 0.0) + jnp.log1p(jnp.exp(-jnp.abs(nl))))
    x3 = x3_scr[...]
    row = lax.broadcasted_iota(jnp.int32, (1, SUBLANES, w), 1)
    if reverse:
        src = x3[:k]
        wrapped = pltpu.roll(jnp.where(row == 0, halo[...], src), SUBLANES - 1, axis=1)
        ext = jnp.concatenate([x3, wrapped], axis=0)
    else:
        src = x3[groups - k:]
        wrapped = pltpu.roll(jnp.where(row == SUBLANES - 1, halo[...], src), 1, axis=1)
        ext = jnp.concatenate([wrapped, x3], axis=0)
    halo[...] = src
    taps = [lw_ref[d, j:j + 1, :] for j in range(LRU_CONV_K)]
    bias = lb_ref[d:d + 1, :]

    def in_time_order(n):
        return range(n - 1, -1, -1) if reverse else range(n)

    h = p = None
    for ci in in_time_order(groups // MIX_CHUNK):
        g0 = ci * MIX_CHUNK
        xc = bias + sum(taps[j] * ext[g0 + j:g0 + j + MIX_CHUNK] for j in range(LRU_CONV_K))
        a, u = _lru_gates(xc.reshape(MIX_CHUNK * SUBLANES, w), log_a_scale, wai_ref, ba_ref, bi_ref, d)
        a3 = a.reshape(MIX_CHUNK, SUBLANES, w)
        u3 = u.reshape(MIX_CHUNK, SUBLANES, w)
        a_scr[g0:g0 + MIX_CHUNK] = a3
        u_scr[g0:g0 + MIX_CHUNK] = u3
        for g in in_time_order(MIX_CHUNK):
            h = u3[g] if h is None else a3[g] * h + u3[g]
            p = a3[g] if p is None else a3[g] * p

    c = carry[0:1, :]
    cs = [None] * SUBLANES
    for r in in_time_order(SUBLANES):
        cs[r] = c
        c = p[r:r + 1, :] * c + h[r:r + 1, :]
    carry[0:1, :] = c

    h = jnp.concatenate(cs, axis=0)
    for g in in_time_order(groups):
        h = a_scr[g] * h + u_scr[g]
        for j in range(w // LANES):
            out_ref[j, pl.ds(g, SUBLANES, stride=SEG_PITCH), :] = h[:, j * LANES:(j + 1) * LANES]
    for r in range(SUBLANES):
        out_ref[:, r * SEG_PITCH + SEG_ROWS:(r + 1) * SEG_PITCH, :] = jnp.zeros(
            (w // LANES, SEG_PITCH - SEG_ROWS, LANES), F32)


def _mixers_kernel(xf_ref, xb_ref, xfn_ref, xbn_ref, perm_ref, lw_ref, lb_ref, wai_ref, ba_ref, bi_ref, lam_ref,
                   hf_ref, hb_ref, x3_f, x3_b, halo_f, halo_b, carry_f, carry_b, a_f, u_f, a_b, u_b):
    def permuted(x_ref):
        x3 = jnp.dot(perm_ref[...], x_ref[...], preferred_element_type=F32)
        return x3.reshape(MIX_T // SUBLANES, SUBLANES, LRU_WIDTH)

    @pl.when(pl.program_id(1) == 0)
    def _():
        halo_f[...] = jnp.zeros(halo_f.shape, F32)
        halo_b[...] = jnp.zeros(halo_b.shape, F32)
        carry_f[...] = jnp.zeros(carry_f.shape, F32)
        carry_b[...] = jnp.zeros(carry_b.shape, F32)

    @pl.when((pl.program_id(0) == 0) & (pl.program_id(1) == 0))
    def _():
        x3_f[...] = permuted(xf_ref)
        x3_b[...] = permuted(xb_ref)

    lru = (lw_ref, lb_ref, wai_ref, ba_ref, bi_ref, lam_ref)
    _lru_direction(x3_f, halo_f, carry_f, a_f, u_f, hf_ref, *lru, 0)
    _lru_direction(x3_b, halo_b, carry_b, a_b, u_b, hb_ref, *lru, 1)
    x3_f[...] = permuted(xfn_ref)
    x3_b[...] = permuted(xbn_ref)


def _mixers(z3, lru_conv_w, lru_conv_b, wai, b_a, b_i, lam, l):
    b, s, _ = z3.shape
    t = MIX_T
    nt = s // t
    slabs = LRU_WIDTH // LANES
    perm = jnp.asarray(_segment_permutation(t), BF16)
    params = (lru_conv_w, lru_conv_b, wai, b_a, b_i, lam)

    def next_tile(bi, i, reverse):
        step = jnp.minimum(bi * nt + i + 1, b * nt - 1)
        ni = step % nt
        return (step // nt, nt - 1 - ni if reverse else ni, COL_LRU_X)

    return pl.pallas_call(
        _mixers_kernel,
        grid=(b, nt),
        in_specs=[
            pl.BlockSpec((None, t, COL), lambda bi, i: (bi, i, COL_LRU_X)),
            pl.BlockSpec((None, t, COL), lambda bi, i: (bi, nt - 1 - i, COL_LRU_X)),
            pl.BlockSpec((None, t, COL), lambda bi, i: next_tile(bi, i, False)),
            pl.BlockSpec((None, t, COL), lambda bi, i: next_tile(bi, i, True)),
            pl.BlockSpec((t, t), lambda bi, i: (0, 0)),
        ] + [_layer_resident(x, l) for x in params],
        out_specs=[
            pl.BlockSpec((slabs, None, TILE_PITCH, LANES), lambda bi, i: (0, bi * nt + i, 0, 0)),
            pl.BlockSpec((slabs, None, TILE_PITCH, LANES), lambda bi, i: (0, bi * nt + nt - 1 - i, 0, 0)),
        ],
        out_shape=[
            jax.ShapeDtypeStruct((slabs, b * nt, TILE_PITCH, LANES), F32),
            jax.ShapeDtypeStruct((slabs, b * nt, TILE_PITCH, LANES), F32),
        ],
        scratch_shapes=[pltpu.VMEM((t // SUBLANES, SUBLANES, LRU_WIDTH), F32)] * 2 + [
            pltpu.VMEM((LRU_CONV_K - 1, SUBLANES, LRU_WIDTH), F32),
            pltpu.VMEM((LRU_CONV_K - 1, SUBLANES, LRU_WIDTH), F32),
            pltpu.VMEM((SUBLANES, LRU_WIDTH), F32),
            pltpu.VMEM((SUBLANES, LRU_WIDTH), F32),
        ] + [pltpu.VMEM((t // SUBLANES, SUBLANES, LRU_WIDTH), F32)] * 4,
        compiler_params=pltpu.CompilerParams(
            dimension_semantics=("arbitrary", "arbitrary"), vmem_limit_bytes=VMEM_LIMIT),
        name="mixers",
    )(z3, z3, z3, z3, perm, *params)


def _outple_kernel(h_ref, ya_ref, hf_ref, hb_ref, zg_ref, zc_ref, gl_ref, zp_ref, zn_ref, p_ref,
                   cw_ref, wo_ref, pn_ref, wg_ref, wp_ref, fn_ref, o_ref, bufc, *, nt, last):
    i = pl.program_id(0)
    pad = SUBLANES
    tm = OUT_TM
    at_start = i % nt == 0
    at_end = i % nt == nt - 1
    before = (zp_ref[:, :COL].astype(F32) * zp_ref[:, COL:].astype(F32))[pad:]
    after = (zn_ref[:, :COL].astype(F32) * zn_ref[:, COL:].astype(F32))[:pad]
    bufc[0:pad, :] = jnp.where(at_start, 0.0, before)
    bufc[pad + tm:2 * pad + tm, :] = jnp.where(at_end, 0.0, after)
    chunk = MIX_CHUNK * SUBLANES
    for r0 in range(0, tm, chunk):
        rows = slice(r0, r0 + chunk)
        bufc[pad + r0:pad + r0 + chunk, :] = zc_ref[rows, :COL].astype(F32) * zc_ref[rows, COL:2 * COL].astype(F32)

    subs = [slice(s * OUT_SUB, (s + 1) * OUT_SUB) for s in range(OUT_TM // OUT_SUB)]
    h1s, pps = [], []
    for rows in subs:
        g_attn = zg_ref[rows, :ATTN_WIDTH].astype(F32)
        y_attn = (ya_ref[rows, :].astype(F32) * _silu(g_attn)).astype(BF16)
        segs = range(rows.start // SEG_ROWS, rows.stop // SEG_ROWS)
        y_lru = jnp.concatenate(
            [jnp.concatenate([hf_ref[j, r * SEG_PITCH:r * SEG_PITCH + SEG_ROWS, :]
                              + hb_ref[j, r * SEG_PITCH:r * SEG_PITCH + SEG_ROWS, :] for r in segs], axis=0)
             for j in range(LRU_WIDTH // LANES)], axis=1)
        y_lru = (y_lru * _silu(gl_ref[rows, :].astype(F32))).astype(BF16)
        conv = sum(cw_ref[j:j + 1, :] * bufc[pad - 1 + j + rows.start:pad - 1 + j + rows.stop, :]
                   for j in range(CONV_K))
        y_conv = zg_ref[rows, ATTN_WIDTH:].astype(F32) * conv * _silu(zc_ref[rows, 2 * COL:].astype(F32))
        y_conv = y_conv.astype(BF16)
        proj = (jnp.dot(y_attn, wo_ref[0:ATTN_WIDTH, :], preferred_element_type=F32)
                + jnp.dot(y_lru, wo_ref[ATTN_WIDTH + CONV_WIDTH:, :], preferred_element_type=F32)
                + jnp.dot(y_conv, wo_ref[ATTN_WIDTH:ATTN_WIDTH + CONV_WIDTH, :], preferred_element_type=F32))
        h1s.append(h_ref[rows, :] + proj)
    for rows in subs:
        pps.append(jnp.dot(p_ref[rows, :].astype(BF16), wp_ref[...], preferred_element_type=F32))
    for rows, h1, pp in zip(subs, h1s, pps):
        n = _rms(h1, pn_ref[...]).astype(BF16)
        gate = _sigmoid(jnp.dot(n, wg_ref[...], preferred_element_type=F32))
        h2 = h1 + gate * pp
        if last:
            h2 = _rms(h2, fn_ref[...])
        o_ref[rows, :] = h2


def _outple(h2d, ya, hf, hb, z2d, p3d, cw, wo_b, pn, wg_b, wp_b, fn, l, nt, last):
    m = h2d.shape[0]
    tm = OUT_TM
    halo = 16
    kern = functools.partial(_outple_kernel, nt=nt, last=last)

    def rows(width, c=0):
        return pl.BlockSpec((tm, width), lambda i: (i, c))

    assert COL_CONV_B == COL_G_ATTN + 2 and COL_G_ATTN % 3 == 0
    assert (COL_CONV_C, COL_CONV_X, COL_G_CONV) == (6, 7, 8)
    halo_before = pl.BlockSpec((halo, 2 * COL), lambda i: (jnp.maximum(i * (tm // halo) - 1, 0), COL_CONV_C // 2))
    halo_after = pl.BlockSpec(
        (halo, 2 * COL), lambda i: (jnp.minimum((i + 1) * (tm // halo), m // halo - 1), COL_CONV_C // 2))

    assert tm == MIX_T and OUT_SUB % SEG_ROWS == 0
    slab = pl.BlockSpec((LRU_WIDTH // LANES, None, TILE_PITCH, LANES), lambda i: (0, i, 0, 0))
    return pl.pallas_call(
        kern,
        grid=(m // tm,),
        in_specs=[
            rows(D_MODEL), rows(ATTN_WIDTH), slab, slab,
            rows(3 * COL, COL_G_ATTN // 3), rows(3 * COL, COL_CONV_C // 3), rows(COL, COL_G_LRU),
            halo_before, halo_after,
            pl.BlockSpec((None, tm, PLE_DIM), lambda i: (l, i, 0)),
            _layer_resident(cw, l), _layer_resident(wo_b, l), _layer_resident(pn, l), _layer_resident(wg_b, l),
            _layer_resident(wp_b, l), _layer_resident(fn, 0),
        ],
        out_specs=rows(D_MODEL),
        out_shape=jax.ShapeDtypeStruct((m, D_MODEL), F32),
        scratch_shapes=[pltpu.VMEM((tm + 2 * SUBLANES, CONV_WIDTH), F32)],
        compiler_params=pltpu.CompilerParams(
            dimension_semantics=("arbitrary",), vmem_limit_bytes=VMEM_LIMIT),
        name="outple",
    )(h2d, ya, hf, hb, *([z2d] * 5), p3d, cw, wo_b, pn, wg_b, wp_b, fn)


def _trunk(x, p, prm, bias):
    b, s, _ = x.shape
    m = b * s
    depth = p.shape[0]
    h = x.reshape(m, D_MODEL)
    p3d = p.reshape(depth, m, PLE_DIM)
    for l in range(depth):
        z = _inproj(h, prm["norm_mix"], prm["w_in"], l)
        z3 = z.reshape(b, s, IN_WIDTH)
        ya = _attention(z3, bias, prm["attn_sink"], l)
        hf, hb = _mixers(z3, prm["lru_conv_w"], prm["lru_conv_b"],
                         prm["wai"], prm["lru_b_a"], prm["lru_b_i"], prm["lru_L"], l)
        h = _outple(h, ya.reshape(m, ATTN_WIDTH), hf, hb, z, p3d,
                    prm["conv_w"], prm["w_out"], prm["ple_norm"], prm["ple_w_gate"], prm["ple_w_proj"],
                    prm["final_norm"], l, s // OUT_TM, last=(l == depth - 1))
    return h.reshape(b, s, D_MODEL)


def kernel(x_prompt, x_sample, p_prompt, p_sample, norm_mix, w_in, w_out, rel_bias, attn_sink, conv_w,
           lru_conv_w, lru_conv_b, lru_w_a, lru_b_a, lru_w_i, lru_b_i, lru_L, ple_norm, ple_w_gate,
           ple_w_proj, final_norm):
    col_scale = jnp.where(jnp.arange(IN_WIDTH) < ATTN_WIDTH, Q_SCALE, 1.0).astype(F32)
    prm = dict(
        norm_mix=norm_mix[:, None, :], w_in=(w_in * col_scale).astype(BF16), w_out=w_out.astype(BF16),
        attn_sink=attn_sink, conv_w=conv_w, lru_conv_w=lru_conv_w, lru_conv_b=lru_conv_b,
        wai=jnp.concatenate([lru_w_a, lru_w_i], axis=-1).astype(BF16),
        lru_b_a=lru_b_a, lru_b_i=lru_b_i, lru_L=lru_L, ple_norm=ple_norm[:, None, :],
        ple_w_gate=ple_w_gate.astype(BF16), ple_w_proj=ple_w_proj.astype(BF16),
        final_norm=final_norm[None, None, :])
    bias = _rel_bias(rel_bias)
    y_prompt = _trunk(x_prompt, p_prompt, prm, bias)
    y_sample = _trunk(x_sample, p_sample, prm, bias)
    return (y_prompt, y_sample)
```

```python
import functools
import math

import numpy as np
import jax
import jax.numpy as jnp
from jax import lax
from jax.experimental import pallas as pl
from jax.experimental.pallas import tpu as pltpu

F32 = jnp.float32
BF16 = jnp.bfloat16

D_MODEL = 2048
N_HEADS = 8
N_KV_HEADS = 2
Q_GROUP = N_HEADS // N_KV_HEADS
HEAD_DIM = 128
ATTN_WIDTH = N_HEADS * HEAD_DIM
KV_WIDTH = N_KV_HEADS * HEAD_DIM
WINDOW = 128
BLOCK = 128
N_BUCKETS = 32
MAX_DISTANCE = 128
CONV_WIDTH = 512
CONV_K = 3
LRU_WIDTH = 512
LRU_HEADS = 4
LRU_BLOCK = 128
LRU_CONV_K = 4
LRU_C = 8.0
MIX_WIDTH = ATTN_WIDTH + CONV_WIDTH + LRU_WIDTH
IN_WIDTH = 2 * ATTN_WIDTH + 2 * KV_WIDTH + 4 * CONV_WIDTH + 2 * LRU_WIDTH
PLE_DIM = 256
NORM_EPS = 1e-6
NEG_INF = -1e30
LOG2E = math.log2(math.e)
Q_SCALE = HEAD_DIM ** -0.5 * LOG2E

COL = 512
COL_KV = 2
COL_G_ATTN = 3
COL_CONV_B = 5
COL_CONV_C = 6
COL_CONV_X = 7
COL_G_CONV = 8
COL_LRU_X = 9
COL_G_LRU = 10

SUBLANES = 8
LANES = 128
VMEM_LIMIT = 60 * 1024 * 1024

INPROJ_TM = 512
INPROJ_TN = 512
ATTN_QB = 8
MIX_T = 512
MIX_CHUNK = 16
SEG_ROWS = MIX_T // SUBLANES
SEG_PITCH = SEG_ROWS + SUBLANES // 2
TILE_PITCH = SUBLANES * SEG_PITCH
OUT_TM = 512
OUT_SUB = 256


def _sigmoid(x):
    return 1.0 / (1.0 + jnp.exp2(x * (-LOG2E)))


def _silu(x):
    return x * _sigmoid(x)


def _rms(x, g):
    ms = jnp.mean(x * x, axis=-1, keepdims=True)
    return x * lax.rsqrt(ms + NORM_EPS) * g


def _layer_resident(x, l):
    nd = x.ndim - 1
    return pl.BlockSpec((None,) + x.shape[1:], lambda *_: (l,) + (0,) * nd, pipeline_mode=pl.Buffered(1))


def _inproj_kernel(h_ref, g_ref, w_ref, z_ref):
    u = _rms(h_ref[...], g_ref[...]).astype(BF16)
    for c in range(IN_WIDTH // INPROJ_TN):
        cols = slice(c * INPROJ_TN, (c + 1) * INPROJ_TN)
        z_ref[:, cols] = jnp.dot(u, w_ref[:, cols], preferred_element_type=F32).astype(z_ref.dtype)


def _inproj(h2d, g, w_b, l):
    m = h2d.shape[0]
    tm = INPROJ_TM
    return pl.pallas_call(
        _inproj_kernel,
        grid=(m // tm,),
        in_specs=[
            pl.BlockSpec((tm, D_MODEL), lambda i: (i, 0)),
            _layer_resident(g, l),
            _layer_resident(w_b, l),
        ],
        out_specs=pl.BlockSpec((tm, IN_WIDTH), lambda i: (i, 0)),
        out_shape=jax.ShapeDtypeStruct((m, IN_WIDTH), BF16),
        compiler_params=pltpu.CompilerParams(
            dimension_semantics=("arbitrary",), vmem_limit_bytes=VMEM_LIMIT),
        name="inproj",
    )(h2d, g, w_b)


def _bucket_table():
    q = np.arange(BLOCK)[:, None]
    c = np.arange(3 * BLOCK)[None, :]
    rel = c - BLOCK - q
    n = np.abs(rel)
    half = N_BUCKETS // 2
    max_exact = half // 2
    n2 = np.maximum(n, 1).astype(np.int64) ** 2
    floor_log2 = np.floor(np.log2(n2.astype(np.float64)) + 1e-9).astype(np.int64)
    large = np.minimum(max_exact + floor_log2 - 6, half - 1)
    bucket = np.where(rel > 0, half, 0) + np.where(n < max_exact, n, large)
    bucket = np.where(n <= WINDOW, bucket, -1)
    return bucket.astype(np.int32)


BIAS_INTERIOR, BIAS_FIRST, BIAS_LAST = 0, 1, 2


def _bias_kernel(tab_ref, bkt_ref, o_ref):
    h = pl.program_id(0)
    b = bkt_ref[...]
    acc = jnp.full(b.shape, NEG_INF, F32)
    for k in range(N_BUCKETS):
        acc = jnp.where(b == k, tab_ref[k, h] * LOG2E, acc)
    col = lax.broadcasted_iota(jnp.int32, b.shape, 1)
    o_ref[BIAS_INTERIOR, 0] = acc
    o_ref[BIAS_FIRST, 0] = jnp.where(col < BLOCK, NEG_INF, acc)
    o_ref[BIAS_LAST, 0] = jnp.where(col >= 2 * BLOCK, NEG_INF, acc)


def _rel_bias(rel_table):
    bkt = jnp.asarray(_bucket_table())
    return pl.pallas_call(
        _bias_kernel,
        grid=(N_HEADS,),
        in_specs=[
            pl.BlockSpec(memory_space=pltpu.SMEM),
            pl.BlockSpec((BLOCK, 3 * BLOCK), lambda h: (0, 0)),
        ],
        out_specs=pl.BlockSpec((3, 1, BLOCK, 3 * BLOCK), lambda h: (0, h, 0, 0)),
        out_shape=jax.ShapeDtypeStruct((3, N_HEADS, BLOCK, 3 * BLOCK), F32),
        name="relbias",
    )(rel_table, bkt)


def _attn_kernel(sink_ref, q_ref, kvp_ref, kvc_ref, kvn_ref, bias_ref, o_ref, *, nblk, layer):
    i = pl.program_id(1)
    ones = jnp.ones((3 * BLOCK, HEAD_DIM), BF16)
    windows = []
    for kh in range(N_KV_HEADS):
        ks = slice(kh * HEAD_DIM, (kh + 1) * HEAD_DIM)
        vs = slice(KV_WIDTH + kh * HEAD_DIM, KV_WIDTH + (kh + 1) * HEAD_DIM)
        windows.append((jnp.concatenate([kvp_ref[:, ks], kvc_ref[:, ks], kvn_ref[:, ks]], axis=0),
                        jnp.concatenate([kvp_ref[:, vs], kvc_ref[:, vs], kvn_ref[:, vs]], axis=0)))
    for kh, (k_win, v_win) in enumerate(windows):
        for s in range(ATTN_QB):
            n = i * ATTN_QB + s
            variant = jnp.where(n == 0, BIAS_FIRST, jnp.where(n == nblk - 1, BIAS_LAST, BIAS_INTERIOR))
            bias4 = bias_ref[variant, kh * Q_GROUP:(kh + 1) * Q_GROUP].reshape(Q_GROUP * BLOCK, 3 * BLOCK)
            rows = slice(s * BLOCK, (s + 1) * BLOCK)
            group_cols = slice(kh * Q_GROUP * HEAD_DIM, (kh + 1) * Q_GROUP * HEAD_DIM)
            q_blk = q_ref[rows, group_cols]
            q4 = jnp.concatenate([q_blk[:, g * HEAD_DIM:(g + 1) * HEAD_DIM] for g in range(Q_GROUP)], axis=0)
            k_sub = k_win[s * BLOCK:s * BLOCK + 3 * BLOCK]
            v_ext = jnp.concatenate([v_win[s * BLOCK:s * BLOCK + 3 * BLOCK], ones], axis=1)
            sc = lax.dot_general(q4, k_sub, (((1,), (1,)), ((), ())), preferred_element_type=F32) + bias4
            es, sink_terms = [], []
            for g in range(Q_GROUP):
                sg = sc[g * BLOCK:(g + 1) * BLOCK]
                sink2 = sink_ref[layer, kh * Q_GROUP + g] * LOG2E
                m = jnp.maximum(jnp.max(sg, axis=-1, keepdims=True), sink2)
                es.append(jnp.exp2(sg - m).astype(BF16))
                sink_terms.append(jnp.exp2(sink2 - m))
            pv = jnp.dot(jnp.concatenate(es, axis=0), v_ext, preferred_element_type=F32)
            outs = []
            for g in range(Q_GROUP):
                pg = pv[g * BLOCK:(g + 1) * BLOCK]
                outs.append(pg[:, :HEAD_DIM] / (pg[:, HEAD_DIM:] + sink_terms[g]))
            o_ref[rows, group_cols] = jnp.concatenate(outs, axis=1).astype(o_ref.dtype)


def _attention(z3, bias, sink, layer):
    b, s, _ = z3.shape
    nblk = s // BLOCK
    assert nblk >= 2 and nblk % ATTN_QB == 0
    tq = ATTN_QB * BLOCK
    kern = functools.partial(_attn_kernel, nblk=nblk, layer=layer)
    return pl.pallas_call(
        kern,
        grid=(b, s // tq),
        in_specs=[
            pl.BlockSpec(memory_space=pltpu.SMEM),
            pl.BlockSpec((None, tq, ATTN_WIDTH), lambda bi, i: (bi, i, 0)),
            pl.BlockSpec((None, BLOCK, COL), lambda bi, i: (bi, jnp.maximum(i * ATTN_QB - 1, 0), COL_KV)),
            pl.BlockSpec((None, tq, COL), lambda bi, i: (bi, i, COL_KV)),
            pl.BlockSpec((None, BLOCK, COL),
                         lambda bi, i: (bi, jnp.minimum((i + 1) * ATTN_QB, nblk - 1), COL_KV)),
            pl.BlockSpec(bias.shape, lambda bi, i: (0, 0, 0, 0)),
        ],
        out_specs=pl.BlockSpec((None, tq, ATTN_WIDTH), lambda bi, i: (bi, i, 0)),
        out_shape=jax.ShapeDtypeStruct((b, s, ATTN_WIDTH), BF16),
        compiler_params=pltpu.CompilerParams(
            dimension_semantics=("arbitrary", "arbitrary"), vmem_limit_bytes=VMEM_LIMIT),
        name="attn",
    )(sink, z3, z3, z3, z3, bias)


def _segment_permutation(t):
    groups = t // SUBLANES
    i = np.arange(t)
    p = np.zeros((t, t), np.float32)
    p[i, (i % SUBLANES) * groups + i // SUBLANES] = 1.0
    return p


def _lru_gates(xc, log_a_scale, wai_ref, ba_ref, bi_ref, d):
    ra, ri = [], []
    for hd in range(LRU_HEADS):
        xh = xc[:, hd * LRU_BLOCK:(hd + 1) * LRU_BLOCK].astype(BF16)
        y = jnp.dot(xh, wai_ref[d, hd], preferred_element_type=F32)
        ra.append(y[:, :LRU_BLOCK])
        ri.append(y[:, LRU_BLOCK:])
    r = _sigmoid(jnp.concatenate(ra, axis=1) + ba_ref[d:d + 1, :])
    gi = _sigmoid(jnp.concatenate(ri, axis=1) + bi_ref[d:d + 1, :])
    log_a = log_a_scale * r
    a = jnp.exp(log_a)
    v = -jnp.tanh(log_a) * (a * a + 1.0)
    root = jnp.where(v > 0.0, v * lax.rsqrt(v), 0.0)
    return a, root * (gi * xc)


def _lru_direction(x3_scr, halo, carry, a_scr, u_scr, out_ref,
                   lw_ref, lb_ref, wai_ref, ba_ref, bi_ref, lam_ref, d):
    reverse = d == 1
    t = MIX_T
    groups = t // SUBLANES
    w = LRU_WIDTH
    k = LRU_CONV_K - 1
    nl = -lam_ref[d:d + 1, :]
    log_a_scale = -LRU_C * (jnp.maximum(nl, 0.0) + jnp.log1p(jnp.exp(-jnp.abs(nl))))
    x3 = x3_scr[...]
    row = lax.broadcasted_iota(jnp.int32, (1, SUBLANES, w), 1)
    if reverse:
        src = x3[:k]
        wrapped = pltpu.roll(jnp.where(row == 0, halo[...], src), SUBLANES - 1, axis=1)
        ext = jnp.concatenate([x3, wrapped], axis=0)
    else:
        src = x3[groups - k:]
        wrapped = pltpu.roll(jnp.where(row == SUBLANES - 1, halo[...], src), 1, axis=1)
        ext = jnp.concatenate([wrapped, x3], axis=0)
    halo[...] = src
    taps = [lw_ref[d, j:j + 1, :] for j in range(LRU_CONV_K)]
    bias = lb_ref[d:d + 1, :]

    def in_time_order(n):
        return range(n - 1, -1, -1) if reverse else range(n)

    h = p = None
    for ci in in_time_order(groups // MIX_CHUNK):
        g0 = ci * MIX_CHUNK
        xc = bias + sum(taps[j] * ext[g0 + j:g0 + j + MIX_CHUNK] for j in range(LRU_CONV_K))
        a, u = _lru_gates(xc.reshape(MIX_CHUNK * SUBLANES, w), log_a_scale, wai_ref, ba_ref, bi_ref, d)
        a3 = a.reshape(MIX_CHUNK, SUBLANES, w)
        u3 = u.reshape(MIX_CHUNK, SUBLANES, w)
        a_scr[g0:g0 + MIX_CHUNK] = a3
        u_scr[g0:g0 + MIX_CHUNK] = u3
        for g in in_time_order(MIX_CHUNK):
            h = u3[g] if h is None else a3[g] * h + u3[g]
            p = a3[g] if p is None else a3[g] * p

    c = carry[0:1, :]
    cs = [None] * SUBLANES
    for r in in_time_order(SUBLANES):
        cs[r] = c
        c = p[r:r + 1, :] * c + h[r:r + 1, :]
    carry[0:1, :] = c

    h = jnp.concatenate(cs, axis=0)
    for g in in_time_order(groups):
        h = a_scr[g] * h + u_scr[g]
        for j in range(w // LANES):
            out_ref[j, pl.ds(g, SUBLANES, stride=SEG_PITCH), :] = h[:, j * LANES:(j + 1) * LANES]
    for r in range(SUBLANES):
        out_ref[:, r * SEG_PITCH + SEG_ROWS:(r + 1) * SEG_PITCH, :] = jnp.zeros(
            (w // LANES, SEG_PITCH - SEG_ROWS, LANES), F32)


def _mixers_kernel(xf_ref, xb_ref, xfn_ref, xbn_ref, perm_ref, lw_ref, lb_ref, wai_ref, ba_ref, bi_ref, lam_ref,
                   hf_ref, hb_ref, x3_f, x3_b, halo_f, halo_b, carry_f, carry_b, a_f, u_f, a_b, u_b):
    def permuted(x_ref):
        x3 = jnp.dot(perm_ref[...], x_ref[...], preferred_element_type=F32)
        return x3.reshape(MIX_T // SUBLANES, SUBLANES, LRU_WIDTH)

    @pl.when(pl.program_id(1) == 0)
    def _():
        halo_f[...] = jnp.zeros(halo_f.shape, F32)
        halo_b[...] = jnp.zeros(halo_b.shape, F32)
        carry_f[...] = jnp.zeros(carry_f.shape, F32)
        carry_b[...] = jnp.zeros(carry_b.shape, F32)

    @pl.when((pl.program_id(0) == 0) & (pl.program_id(1) == 0))
    def _():
        x3_f[...] = permuted(xf_ref)
        x3_b[...] = permuted(xb_ref)

    lru = (lw_ref, lb_ref, wai_ref, ba_ref, bi_ref, lam_ref)
    _lru_direction(x3_f, halo_f, carry_f, a_f, u_f, hf_ref, *lru, 0)
    _lru_direction(x3_b, halo_b, carry_b, a_b, u_b, hb_ref, *lru, 1)
    x3_f[...] = permuted(xfn_ref)
    x3_b[...] = permuted(xbn_ref)


def _mixers(z3, lru_conv_w, lru_conv_b, wai, b_a, b_i, lam, l):
    b, s, _ = z3.shape
    t = MIX_T
    nt = s // t
    slabs = LRU_WIDTH // LANES
    perm = jnp.asarray(_segment_permutation(t), BF16)
    params = (lru_conv_w, lru_conv_b, wai, b_a, b_i, lam)

    def next_tile(bi, i, reverse):
        step = jnp.minimum(bi * nt + i + 1, b * nt - 1)
        ni = step % nt
        return (step // nt, nt - 1 - ni if reverse else ni, COL_LRU_X)

    return pl.pallas_call(
        _mixers_kernel,
        grid=(b, nt),
        in_specs=[
            pl.BlockSpec((None, t, COL), lambda bi, i: (bi, i, COL_LRU_X)),
            pl.BlockSpec((None, t, COL), lambda bi, i: (bi, nt - 1 - i, COL_LRU_X)),
            pl.BlockSpec((None, t, COL), lambda bi, i: next_tile(bi, i, False)),
            pl.BlockSpec((None, t, COL), lambda bi, i: next_tile(bi, i, True)),
            pl.BlockSpec((t, t), lambda bi, i: (0, 0)),
        ] + [_layer_resident(x, l) for x in params],
        out_specs=[
            pl.BlockSpec((slabs, None, TILE_PITCH, LANES), lambda bi, i: (0, bi * nt + i, 0, 0)),
            pl.BlockSpec((slabs, None, TILE_PITCH, LANES), lambda bi, i: (0, bi * nt + nt - 1 - i, 0, 0)),
        ],
        out_shape=[
            jax.ShapeDtypeStruct((slabs, b * nt, TILE_PITCH, LANES), F32),
            jax.ShapeDtypeStruct((slabs, b * nt, TILE_PITCH, LANES), F32),
        ],
        scratch_shapes=[pltpu.VMEM((t // SUBLANES, SUBLANES, LRU_WIDTH), F32)] * 2 + [
            pltpu.VMEM((LRU_CONV_K - 1, SUBLANES, LRU_WIDTH), F32),
            pltpu.VMEM((LRU_CONV_K - 1, SUBLANES, LRU_WIDTH), F32),
            pltpu.VMEM((SUBLANES, LRU_WIDTH), F32),
            pltpu.VMEM((SUBLANES, LRU_WIDTH), F32),
        ] + [pltpu.VMEM((t // SUBLANES, SUBLANES, LRU_WIDTH), F32)] * 4,
        compiler_params=pltpu.CompilerParams(
            dimension_semantics=("arbitrary", "arbitrary"), vmem_limit_bytes=VMEM_LIMIT),
        name="mixers",
    )(z3, z3, z3, z3, perm, *params)


def _outple_kernel(h_ref, ya_ref, hf_ref, hb_ref, zg_ref, zc_ref, gl_ref, zp_ref, zn_ref, p_ref,
                   cw_ref, wo_ref, pn_ref, wg_ref, wp_ref, fn_ref, o_ref, bufc, *, nt, last):
    i = pl.program_id(0)
    pad = SUBLANES
    tm = OUT_TM
    at_start = i % nt == 0
    at_end = i % nt == nt - 1
    before = (zp_ref[:, :COL].astype(F32) * zp_ref[:, COL:].astype(F32))[pad:]
    after = (zn_ref[:, :COL].astype(F32) * zn_ref[:, COL:].astype(F32))[:pad]
    bufc[0:pad, :] = jnp.where(at_start, 0.0, before)
    bufc[pad + tm:2 * pad + tm, :] = jnp.where(at_end, 0.0, after)
    chunk = MIX_CHUNK * SUBLANES
    for r0 in range(0, tm, chunk):
        rows = slice(r0, r0 + chunk)
        bufc[pad + r0:pad + r0 + chunk, :] = zc_ref[rows, :COL].astype(F32) * zc_ref[rows, COL:2 * COL].astype(F32)

    subs = [slice(s * OUT_SUB, (s + 1) * OUT_SUB) for s in range(OUT_TM // OUT_SUB)]
    h1s, pps = [], []
    for rows in subs:
        g_attn = zg_ref[rows, :ATTN_WIDTH].astype(F32)
        y_attn = (ya_ref[rows, :].astype(F32) * _silu(g_attn)).astype(BF16)
        segs = range(rows.start // SEG_ROWS, rows.stop // SEG_ROWS)
        y_lru = jnp.concatenate(
            [jnp.concatenate([hf_ref[j, r * SEG_PITCH:r * SEG_PITCH + SEG_ROWS, :]
                              + hb_ref[j, r * SEG_PITCH:r * SEG_PITCH + SEG_ROWS, :] for r in segs], axis=0)
             for j in range(LRU_WIDTH // LANES)], axis=1)
        y_lru = (y_lru * _silu(gl_ref[rows, :].astype(F32))).astype(BF16)
        conv = sum(cw_ref[j:j + 1, :] * bufc[pad - 1 + j + rows.start:pad - 1 + j + rows.stop, :]
                   for j in range(CONV_K))
        y_conv = zg_ref[rows, ATTN_WIDTH:].astype(F32) * conv * _silu(zc_ref[rows, 2 * COL:].astype(F32))
        y_conv = y_conv.astype(BF16)
        proj = (jnp.dot(y_attn, wo_ref[0:ATTN_WIDTH, :], preferred_element_type=F32)
                + jnp.dot(y_lru, wo_ref[ATTN_WIDTH + CONV_WIDTH:, :], preferred_element_type=F32)
                + jnp.dot(y_conv, wo_ref[ATTN_WIDTH:ATTN_WIDTH + CONV_WIDTH, :], preferred_element_type=F32))
        h1s.append(h_ref[rows, :] + proj)
    for rows in subs:
        pps.append(jnp.dot(p_ref[rows, :].astype(BF16), wp_ref[...], preferred_element_type=F32))
    for rows, h1, pp in zip(subs, h1s, pps):
        n = _rms(h1, pn_ref[...]).astype(BF16)
        gate = _sigmoid(jnp.dot(n, wg_ref[...], preferred_element_type=F32))
        h2 = h1 + gate * pp
        if last:
            h2 = _rms(h2, fn_ref[...])
        o_ref[rows, :] = h2


def _outple(h2d, ya, hf, hb, z2d, p3d, cw, wo_b, pn, wg_b, wp_b, fn, l, nt, last):
    m = h2d.shape[0]
    tm = OUT_TM
    halo = 16
    kern = functools.partial(_outple_kernel, nt=nt, last=last)

    def rows(width, c=0):
        return pl.BlockSpec((tm, width), lambda i: (i, c))

    assert COL_CONV_B == COL_G_ATTN + 2 and COL_G_ATTN % 3 == 0
    assert (COL_CONV_C, COL_CONV_X, COL_G_CONV) == (6, 7, 8)
    halo_before = pl.BlockSpec((halo, 2 * COL), lambda i: (jnp.maximum(i * (tm // halo) - 1, 0), COL_CONV_C // 2))
    halo_after = pl.BlockSpec(
        (halo, 2 * COL), lambda i: (jnp.minimum((i + 1) * (tm // halo), m // halo - 1), COL_CONV_C // 2))

    assert tm == MIX_T and OUT_SUB % SEG_ROWS == 0
    slab = pl.BlockSpec((LRU_WIDTH // LANES, None, TILE_PITCH, LANES), lambda i: (0, i, 0, 0))
    return pl.pallas_call(
        kern,
        grid=(m // tm,),
        in_specs=[
            rows(D_MODEL), rows(ATTN_WIDTH), slab, slab,
            rows(3 * COL, COL_G_ATTN // 3), rows(3 * COL, COL_CONV_C // 3), rows(COL, COL_G_LRU),
            halo_before, halo_after,
            pl.BlockSpec((None, tm, PLE_DIM), lambda i: (l, i, 0)),
            _layer_resident(cw, l), _layer_resident(wo_b, l), _layer_resident(pn, l), _layer_resident(wg_b, l),
            _layer_resident(wp_b, l), _layer_resident(fn, 0),
        ],
        out_specs=rows(D_MODEL),
        out_shape=jax.ShapeDtypeStruct((m, D_MODEL), F32),
        scratch_shapes=[pltpu.VMEM((tm + 2 * SUBLANES, CONV_WIDTH), F32)],
        compiler_params=pltpu.CompilerParams(
            dimension_semantics=("arbitrary",), vmem_limit_bytes=VMEM_LIMIT),
        name="outple",
    )(h2d, ya, hf, hb, *([z2d] * 5), p3d, cw, wo_b, pn, wg_b, wp_b, fn)


def _trunk(x, p, prm, bias):
    b, s, _ = x.shape
    m = b * s
    depth = p.shape[0]
    h = x.reshape(m, D_MODEL)
    p3d = p.reshape(depth, m, PLE_DIM)
    for l in range(depth):
        z = _inproj(h, prm["norm_mix"], prm["w_in"], l)
        z3 = z.reshape(b, s, IN_WIDTH)
        ya = _attention(z3, bias, prm["attn_sink"], l)
        hf, hb = _mixers(z3, prm["lru_conv_w"], prm["lru_conv_b"],
                         prm["wai"], prm["lru_b_a"], prm["lru_b_i"], prm["lru_L"], l)
        h = _outple(h, ya.reshape(m, ATTN_WIDTH), hf, hb, z, p3d,
                    prm["conv_w"], prm["w_out"], prm["ple_norm"], prm["ple_w_gate"], prm["ple_w_proj"],
                    prm["final_norm"], l, s // OUT_TM, last=(l == depth - 1))
    return h.reshape(b, s, D_MODEL)


def kernel(x_prompt, x_sample, p_prompt, p_sample, norm_mix, w_in, w_out, rel_bias, attn_sink, conv_w,
           lru_conv_w, lru_conv_b, lru_w_a, lru_b_a, lru_w_i, lru_b_i, lru_L, ple_norm, ple_w_gate,
           ple_w_proj, final_norm):
    col_scale = jnp.where(jnp.arange(IN_WIDTH) < ATTN_WIDTH, Q_SCALE, 1.0).astype(F32)
    prm = dict(
        norm_mix=norm_mix[:, None, :], w_in=(w_in * col_scale).astype(BF16), w_out=w_out.astype(BF16),
        attn_sink=attn_sink, conv_w=conv_w, lru_conv_w=lru_conv_w, lru_conv_b=lru_conv_b,
        wai=jnp.concatenate([lru_w_a, lru_w_i], axis=-1).astype(BF16),
        lru_b_a=lru_b_a, lru_b_i=lru_b_i, lru_L=lru_L, ple_norm=ple_norm[:, None, :],
        ple_w_gate=ple_w_gate.astype(BF16), ple_w_proj=ple_w_proj.astype(BF16),
        final_norm=final_norm[None, None, :])
    bias = _rel_bias(rel_bias)
    y_prompt = _trunk(x_prompt, p_prompt, prm, bias)
    y_sample = _trunk(x_sample, p_sample, prm, bias)
    return (y_prompt, y_sample)
```

```python
import functools
import math

import numpy as np
import jax
import jax.numpy as jnp
from jax import lax
from jax.experimental import pallas as pl
from jax.experimental.pallas import tpu as pltpu

F32 = jnp.float32
BF16 = jnp.bfloat16

D_MODEL = 2048
N_HEADS = 8
N_KV_HEADS = 2
Q_GROUP = N_HEADS // N_KV_HEADS
HEAD_DIM = 128
ATTN_WIDTH = N_HEADS * HEAD_DIM
KV_WIDTH = N_KV_HEADS * HEAD_DIM
WINDOW = 128
BLOCK = 128
N_BUCKETS = 32
MAX_DISTANCE = 128
CONV_WIDTH = 512
CONV_K = 3
LRU_WIDTH = 512
LRU_HEADS = 4
LRU_BLOCK = 128
LRU_CONV_K = 4
LRU_C = 8.0
MIX_WIDTH = ATTN_WIDTH + CONV_WIDTH + LRU_WIDTH
IN_WIDTH = 2 * ATTN_WIDTH + 2 * KV_WIDTH + 4 * CONV_WIDTH + 2 * LRU_WIDTH
PLE_DIM = 256
NORM_EPS = 1e-6
NEG_INF = -1e30
LOG2E = math.log2(math.e)
Q_SCALE = HEAD_DIM ** -0.5 * LOG2E

COL = 512
COL_KV = 2
COL_G_ATTN = 3
COL_CONV_B = 5
COL_CONV_C = 6
COL_CONV_X = 7
COL_G_CONV = 8
COL_LRU_X = 9
COL_G_LRU = 10

SUBLANES = 8
LANES = 128
VMEM_LIMIT = 60 * 1024 * 1024

INPROJ_TM = 512
INPROJ_TN = 512
ATTN_QB = 16
MIX_T = 512
MIX_CHUNK = 16
SEG_ROWS = MIX_T // SUBLANES
SEG_PITCH = SEG_ROWS + SUBLANES // 2
TILE_PITCH = SUBLANES * SEG_PITCH
OUT_TM = 512
OUT_SUB = 256


def _sigmoid(x):
    return 1.0 / (1.0 + jnp.exp2(x * (-LOG2E)))


def _silu(x):
    return x * _sigmoid(x)


def _rms(x, g):
    ms = jnp.mean(x * x, axis=-1, keepdims=True)
    return x * lax.rsqrt(ms + NORM_EPS) * g


def _layer_resident(x, l):
    nd = x.ndim - 1
    return pl.BlockSpec((None,) + x.shape[1:], lambda *_: (l,) + (0,) * nd, pipeline_mode=pl.Buffered(1))


def _inproj_kernel(h_ref, g_ref, w_ref, z_ref):
    u = _rms(h_ref[...], g_ref[...]).astype(BF16)
    for c in range(IN_WIDTH // INPROJ_TN):
        cols = slice(c * INPROJ_TN, (c + 1) * INPROJ_TN)
        z_ref[:, cols] = jnp.dot(u, w_ref[:, cols], preferred_element_type=F32).astype(z_ref.dtype)


def _inproj(h2d, g, w_b, l):
    m = h2d.shape[0]
    tm = INPROJ_TM
    return pl.pallas_call(
        _inproj_kernel,
        grid=(m // tm,),
        in_specs=[
            pl.BlockSpec((tm, D_MODEL), lambda i: (i, 0)),
            _layer_resident(g, l),
            _layer_resident(w_b, l),
        ],
        out_specs=pl.BlockSpec((tm, IN_WIDTH), lambda i: (i, 0)),
        out_shape=jax.ShapeDtypeStruct((m, IN_WIDTH), BF16),
        compiler_params=pltpu.CompilerParams(
            dimension_semantics=("arbitrary",), vmem_limit_bytes=VMEM_LIMIT),
        name="inproj",
    )(h2d, g, w_b)


def _bucket_table():
    q = np.arange(BLOCK)[:, None]
    c = np.arange(3 * BLOCK)[None, :]
    rel = c - BLOCK - q
    n = np.abs(rel)
    half = N_BUCKETS // 2
    max_exact = half // 2
    n2 = np.maximum(n, 1).astype(np.int64) ** 2
    floor_log2 = np.floor(np.log2(n2.astype(np.float64)) + 1e-9).astype(np.int64)
    large = np.minimum(max_exact + floor_log2 - 6, half - 1)
    bucket = np.where(rel > 0, half, 0) + np.where(n < max_exact, n, large)
    bucket = np.where(n <= WINDOW, bucket, -1)
    return bucket.astype(np.int32)


BIAS_INTERIOR, BIAS_FIRST, BIAS_LAST = 0, 1, 2


def _bias_kernel(tab_ref, bkt_ref, o_ref):
    h = pl.program_id(0)
    b = bkt_ref[...]
    acc = jnp.full(b.shape, NEG_INF, F32)
    for k in range(N_BUCKETS):
        acc = jnp.where(b == k, tab_ref[k, h] * LOG2E, acc)
    col = lax.broadcasted_iota(jnp.int32, b.shape, 1)
    o_ref[BIAS_INTERIOR, 0] = acc
    o_ref[BIAS_FIRST, 0] = jnp.where(col < BLOCK, NEG_INF, acc)
    o_ref[BIAS_LAST, 0] = jnp.where(col >= 2 * BLOCK, NEG_INF, acc)


def _rel_bias(rel_table):
    bkt = jnp.asarray(_bucket_table())
    return pl.pallas_call(
        _bias_kernel,
        grid=(N_HEADS,),
        in_specs=[
            pl.BlockSpec(memory_space=pltpu.SMEM),
            pl.BlockSpec((BLOCK, 3 * BLOCK), lambda h: (0, 0)),
        ],
        out_specs=pl.BlockSpec((3, 1, BLOCK, 3 * BLOCK), lambda h: (0, h, 0, 0)),
        out_shape=jax.ShapeDtypeStruct((3, N_HEADS, BLOCK, 3 * BLOCK), F32),
        name="relbias",
    )(rel_table, bkt)


def _attn_kernel(sink_ref, q_ref, kvp_ref, kvc_ref, kvn_ref, bias_ref, o_ref, *, nblk, layer):
    i = pl.program_id(1)
    ones = jnp.ones((3 * BLOCK, HEAD_DIM), BF16)
    windows = []
    for kh in range(N_KV_HEADS):
        ks = slice(kh * HEAD_DIM, (kh + 1) * HEAD_DIM)
        vs = slice(KV_WIDTH + kh * HEAD_DIM, KV_WIDTH + (kh + 1) * HEAD_DIM)
        windows.append((jnp.concatenate([kvp_ref[:, ks], kvc_ref[:, ks], kvn_ref[:, ks]], axis=0),
                        jnp.concatenate([kvp_ref[:, vs], kvc_ref[:, vs], kvn_ref[:, vs]], axis=0)))
    for kh, (k_win, v_win) in enumerate(windows):
        for s in range(ATTN_QB):
            n = i * ATTN_QB + s
            variant = jnp.where(n == 0, BIAS_FIRST, jnp.where(n == nblk - 1, BIAS_LAST, BIAS_INTERIOR))
            bias4 = bias_ref[variant, kh * Q_GROUP:(kh + 1) * Q_GROUP].reshape(Q_GROUP * BLOCK, 3 * BLOCK)
            rows = slice(s * BLOCK, (s + 1) * BLOCK)
            group_cols = slice(kh * Q_GROUP * HEAD_DIM, (kh + 1) * Q_GROUP * HEAD_DIM)
            q_blk = q_ref[rows, group_cols]
            q4 = jnp.concatenate([q_blk[:, g * HEAD_DIM:(g + 1) * HEAD_DIM] for g in range(Q_GROUP)], axis=0)
            k_sub = k_win[s * BLOCK:s * BLOCK + 3 * BLOCK]
            v_ext = jnp.concatenate([v_win[s * BLOCK:s * BLOCK + 3 * BLOCK], ones], axis=1)
            sc = lax.dot_general(q4, k_sub, (((1,), (1,)), ((), ())), preferred_element_type=F32) + bias4
            es, sink_terms = [], []
            for g in range(Q_GROUP):
                sg = sc[g * BLOCK:(g + 1) * BLOCK]
                sink2 = sink_ref[layer, kh * Q_GROUP + g] * LOG2E
                m = jnp.maximum(jnp.max(sg, axis=-1, keepdims=True), sink2)
                es.append(jnp.exp2(sg - m).astype(BF16))
                sink_terms.append(jnp.exp2(sink2 - m))
            pv = jnp.dot(jnp.concatenate(es, axis=0), v_ext, preferred_element_type=F32)
            outs = []
            for g in range(Q_GROUP):
                pg = pv[g * BLOCK:(g + 1) * BLOCK]
                outs.append(pg[:, :HEAD_DIM] / (pg[:, HEAD_DIM:] + sink_terms[g]))
            o_ref[rows, group_cols] = jnp.concatenate(outs, axis=1).astype(o_ref.dtype)


def _attention(z3, bias, sink, layer):
    b, s, _ = z3.shape
    nblk = s // BLOCK
    assert nblk >= 2 and nblk % ATTN_QB == 0
    tq = ATTN_QB * BLOCK
    kern = functools.partial(_attn_kernel, nblk=nblk, layer=layer)
    return pl.pallas_call(
        kern,
        grid=(b, s // tq),
        in_specs=[
            pl.BlockSpec(memory_space=pltpu.SMEM),
            pl.BlockSpec((None, tq, ATTN_WIDTH), lambda bi, i: (bi, i, 0)),
            pl.BlockSpec((None, BLOCK, COL), lambda bi, i: (bi, jnp.maximum(i * ATTN_QB - 1, 0), COL_KV)),
            pl.BlockSpec((None, tq, COL), lambda bi, i: (bi, i, COL_KV)),
            pl.BlockSpec((None, BLOCK, COL),
                         lambda bi, i: (bi, jnp.minimum((i + 1) * ATTN_QB, nblk - 1), COL_KV)),
            pl.BlockSpec(bias.shape, lambda bi, i: (0, 0, 0, 0)),
        ],
        out_specs=pl.BlockSpec((None, tq, ATTN_WIDTH), lambda bi, i: (bi, i, 0)),
        out_shape=jax.ShapeDtypeStruct((b, s, ATTN_WIDTH), BF16),
        compiler_params=pltpu.CompilerParams(
            dimension_semantics=("arbitrary", "arbitrary"), vmem_limit_bytes=VMEM_LIMIT),
        name="attn",
    )(sink, z3, z3, z3, z3, bias)


def _segment_permutation(t):
    groups = t // SUBLANES
    i = np.arange(t)
    p = np.zeros((t, t), np.float32)
    p[i, (i % SUBLANES) * groups + i // SUBLANES] = 1.0
    return p


def _lru_gates(xc, log_a_scale, wai_ref, ba_ref, bi_ref, d):
    ra, ri = [], []
    for hd in range(LRU_HEADS):
        xh = xc[:, hd * LRU_BLOCK:(hd + 1) * LRU_BLOCK].astype(BF16)
        y = jnp.dot(xh, wai_ref[d, hd], preferred_element_type=F32)
        ra.append(y[:, :LRU_BLOCK])
        ri.append(y[:, LRU_BLOCK:])
    r = _sigmoid(jnp.concatenate(ra, axis=1) + ba_ref[d:d + 1, :])
    gi = _sigmoid(jnp.concatenate(ri, axis=1) + bi_ref[d:d + 1, :])
    log_a = log_a_scale * r
    a = jnp.exp(log_a)
    v = -jnp.tanh(log_a) * (a * a + 1.0)
    root = jnp.where(v > 0.0, v * lax.rsqrt(v), 0.0)
    return a, root * (gi * xc)


def _lru_direction(x3_scr, halo, carry, a_scr, u_scr, out_ref,
                   lw_ref, lb_ref, wai_ref, ba_ref, bi_ref, lam_ref, d):
    reverse = d == 1
    t = MIX_T
    groups = t // SUBLANES
    w = LRU_WIDTH
    k = LRU_CONV_K - 1
    nl = -lam_ref[d:d + 1, :]
    log_a_scale = -LRU_C * (jnp.maximum(nl, 0.0) + jnp.log1p(jnp.exp(-jnp.abs(nl))))
    x3 = x3_scr[...]
    row = lax.broadcasted_iota(jnp.int32, (1, SUBLANES, w), 1)
    if reverse:
        src = x3[:k]
        wrapped = pltpu.roll(jnp.where(row == 0, halo[...], src), SUBLANES - 1, axis=1)
        ext = jnp.concatenate([x3, wrapped], axis=0)
    else:
        src = x3[groups - k:]
        wrapped = pltpu.roll(jnp.where(row == SUBLANES - 1, halo[...], src), 1, axis=1)
        ext = jnp.concatenate([wrapped, x3], axis=0)
    halo[...] = src
    taps = [lw_ref[d, j:j + 1, :] for j in range(LRU_CONV_K)]
    bias = lb_ref[d:d + 1, :]

    def in_time_order(n):
        return range(n - 1, -1, -1) if reverse else range(n)

    h = p = None
    for ci in in_time_order(groups // MIX_CHUNK):
        g0 = ci * MIX_CHUNK
        xc = bias + sum(taps[j] * ext[g0 + j:g0 + j + MIX_CHUNK] for j in range(LRU_CONV_K))
        a, u = _lru_gates(xc.reshape(MIX_CHUNK * SUBLANES, w), log_a_scale, wai_ref, ba_ref, bi_ref, d)
        a3 = a.reshape(MIX_CHUNK, SUBLANES, w)
        u3 = u.reshape(MIX_CHUNK, SUBLANES, w)
        a_scr[g0:g0 + MIX_CHUNK] = a3
        u_scr[g0:g0 + MIX_CHUNK] = u3
        for g in in_time_order(MIX_CHUNK):
            h = u3[g] if h is None else a3[g] * h + u3[g]
            p = a3[g] if p is None else a3[g] * p

    c = carry[0:1, :]
    cs = [None] * SUBLANES
    for r in in_time_order(SUBLANES):
        cs[r] = c
        c = p[r:r + 1, :] * c + h[r:r + 1, :]
    carry[0:1, :] = c

    h = jnp.concatenate(cs, axis=0)
    for g in in_time_order(groups):
        h = a_scr[g] * h + u_scr[g]
        for j in range(w // LANES):
            out_ref[j, pl.ds(g, SUBLANES, stride=SEG_PITCH), :] = h[:, j * LANES:(j + 1) * LANES]
    for r in range(SUBLANES):
        out_ref[:, r * SEG_PITCH + SEG_ROWS:(r + 1) * SEG_PITCH, :] = jnp.zeros(
            (w // LANES, SEG_PITCH - SEG_ROWS, LANES), F32)


def _mixers_kernel(xf_ref, xb_ref, xfn_ref, xbn_ref, perm_ref, lw_ref, lb_ref, wai_ref, ba_ref, bi_ref, lam_ref,
                   hf_ref, hb_ref, x3_f, x3_b, halo_f, halo_b, carry_f, carry_b, a_f, u_f, a_b, u_b):
    def permuted(x_ref):
        x3 = jnp.dot(perm_ref[...], x_ref[...], preferred_element_type=F32)
        return x3.reshape(MIX_T // SUBLANES, SUBLANES, LRU_WIDTH)

    @pl.when(pl.program_id(1) == 0)
    def _():
        halo_f[...] = jnp.zeros(halo_f.shape, F32)
        halo_b[...] = jnp.zeros(halo_b.shape, F32)
        carry_f[...] = jnp.zeros(carry_f.shape, F32)
        carry_b[...] = jnp.zeros(carry_b.shape, F32)

    @pl.when((pl.program_id(0) == 0) & (pl.program_id(1) == 0))
    def _():
        x3_f[...] = permuted(xf_ref)
        x3_b[...] = permuted(xb_ref)

    lru = (lw_ref, lb_ref, wai_ref, ba_ref, bi_ref, lam_ref)
    _lru_direction(x3_f, halo_f, carry_f, a_f, u_f, hf_ref, *lru, 0)
    _lru_direction(x3_b, halo_b, carry_b, a_b, u_b, hb_ref, *lru, 1)
    x3_f[...] = permuted(xfn_ref)
    x3_b[...] = permuted(xbn_ref)


def _mixers(z3, lru_conv_w, lru_conv_b, wai, b_a, b_i, lam, l):
    b, s, _ = z3.shape
    t = MIX_T
    nt = s // t
    slabs = LRU_WIDTH // LANES
    perm = jnp.asarray(_segment_permutation(t), BF16)
    params = (lru_conv_w, lru_conv_b, wai, b_a, b_i, lam)

    def next_tile(bi, i, reverse):
        step = jnp.minimum(bi * nt + i + 1, b * nt - 1)
        ni = step % nt
        return (step // nt, nt - 1 - ni if reverse else ni, COL_LRU_X)

    return pl.pallas_call(
        _mixers_kernel,
        grid=(b, nt),
        in_specs=[
            pl.BlockSpec((None, t, COL), lambda bi, i: (bi, i, COL_LRU_X)),
            pl.BlockSpec((None, t, COL), lambda bi, i: (bi, nt - 1 - i, COL_LRU_X)),
            pl.BlockSpec((None, t, COL), lambda bi, i: next_tile(bi, i, False)),
            pl.BlockSpec((None, t, COL), lambda bi, i: next_tile(bi, i, True)),
            pl.BlockSpec((t, t), lambda bi, i: (0, 0)),
        ] + [_layer_resident(x, l) for x in params],
        out_specs=[
            pl.BlockSpec((slabs, None, TILE_PITCH, LANES), lambda bi, i: (0, bi * nt + i, 0, 0)),
            pl.BlockSpec((slabs, None, TILE_PITCH, LANES), lambda bi, i: (0, bi * nt + nt - 1 - i, 0, 0)),
        ],
        out_shape=[
            jax.ShapeDtypeStruct((slabs, b * nt, TILE_PITCH, LANES), F32),
            jax.ShapeDtypeStruct((slabs, b * nt, TILE_PITCH, LANES), F32),
        ],
        scratch_shapes=[pltpu.VMEM((t // SUBLANES, SUBLANES, LRU_WIDTH), F32)] * 2 + [
            pltpu.VMEM((LRU_CONV_K - 1, SUBLANES, LRU_WIDTH), F32),
            pltpu.VMEM((LRU_CONV_K - 1, SUBLANES, LRU_WIDTH), F32),
            pltpu.VMEM((SUBLANES, LRU_WIDTH), F32),
            pltpu.VMEM((SUBLANES, LRU_WIDTH), F32),
        ] + [pltpu.VMEM((t // SUBLANES, SUBLANES, LRU_WIDTH), F32)] * 4,
        compiler_params=pltpu.CompilerParams(
            dimension_semantics=("arbitrary", "arbitrary"), vmem_limit_bytes=VMEM_LIMIT),
        name="mixers",
    )(z3, z3, z3, z3, perm, *params)


def _outple_kernel(h_ref, ya_ref, hf_ref, hb_ref, zg_ref, zc_ref, gl_ref, zp_ref, zn_ref, p_ref,
                   cw_ref, wo_ref, pn_ref, wg_ref, wp_ref, fn_ref, o_ref, bufc, *, nt, last):
    i = pl.program_id(0)
    pad = SUBLANES
    tm = OUT_TM
    at_start = i % nt == 0
    at_end = i % nt == nt - 1
    before = (zp_ref[:, :COL].astype(F32) * zp_ref[:, COL:].astype(F32))[pad:]
    after = (zn_ref[:, :COL].astype(F32) * zn_ref[:, COL:].astype(F32))[:pad]
    bufc[0:pad, :] = jnp.where(at_start, 0.0, before)
    bufc[pad + tm:2 * pad + tm, :] = jnp.where(at_end, 0.0, after)
    chunk = MIX_CHUNK * SUBLANES
    for r0 in range(0, tm, chunk):
        rows = slice(r0, r0 + chunk)
        bufc[pad + r0:pad + r0 + chunk, :] = zc_ref[rows, :COL].astype(F32) * zc_ref[rows, COL:2 * COL].astype(F32)

    subs = [slice(s * OUT_SUB, (s + 1) * OUT_SUB) for s in range(OUT_TM // OUT_SUB)]
    h1s, pps = [], []
    for rows in subs:
        g_attn = zg_ref[rows, :ATTN_WIDTH].astype(F32)
        y_attn = (ya_ref[rows, :].astype(F32) * _silu(g_attn)).astype(BF16)
        segs = range(rows.start // SEG_ROWS, rows.stop // SEG_ROWS)
        y_lru = jnp.concatenate(
            [jnp.concatenate([hf_ref[j, r * SEG_PITCH:r * SEG_PITCH + SEG_ROWS, :]
                              + hb_ref[j, r * SEG_PITCH:r * SEG_PITCH + SEG_ROWS, :] for r in segs], axis=0)
             for j in range(LRU_WIDTH // LANES)], axis=1)
        y_lru = (y_lru * _silu(gl_ref[rows, :].astype(F32))).astype(BF16)
        conv = sum(cw_ref[j:j + 1, :] * bufc[pad - 1 + j + rows.start:pad - 1 + j + rows.stop, :]
                   for j in range(CONV_K))
        y_conv = zg_ref[rows, ATTN_WIDTH:].astype(F32) * conv * _silu(zc_ref[rows, 2 * COL:].astype(F32))
        y_conv = y_conv.astype(BF16)
        proj = (jnp.dot(y_attn, wo_ref[0:ATTN_WIDTH, :], preferred_element_type=F32)
                + jnp.dot(y_lru, wo_ref[ATTN_WIDTH + CONV_WIDTH:, :], preferred_element_type=F32)
                + jnp.dot(y_conv, wo_ref[ATTN_WIDTH:ATTN_WIDTH + CONV_WIDTH, :], preferred_element_type=F32))
        h1s.append(h_ref[rows, :] + proj)
    for rows in subs:
        pps.append(jnp.dot(p_ref[rows, :].astype(BF16), wp_ref[...], preferred_element_type=F32))
    for rows, h1, pp in zip(subs, h1s, pps):
        n = _rms(h1, pn_ref[...]).astype(BF16)
        gate = _sigmoid(jnp.dot(n, wg_ref[...], preferred_element_type=F32))
        h2 = h1 + gate * pp
        if last:
            h2 = _rms(h2, fn_ref[...])
        o_ref[rows, :] = h2


def _outple(h2d, ya, hf, hb, z2d, p3d, cw, wo_b, pn, wg_b, wp_b, fn, l, nt, last):
    m = h2d.shape[0]
    tm = OUT_TM
    halo = 16
    kern = functools.partial(_outple_kernel, nt=nt, last=last)

    def rows(width, c=0):
        return pl.BlockSpec((tm, width), lambda i: (i, c))

    assert COL_CONV_B == COL_G_ATTN + 2 and COL_G_ATTN % 3 == 0
    assert (COL_CONV_C, COL_CONV_X, COL_G_CONV) == (6, 7, 8)
    halo_before = pl.BlockSpec((halo, 2 * COL), lambda i: (jnp.maximum(i * (tm // halo) - 1, 0), COL_CONV_C // 2))
    halo_after = pl.BlockSpec(
        (halo, 2 * COL), lambda i: (jnp.minimum((i + 1) * (tm // halo), m // halo - 1), COL_CONV_C // 2))

    assert tm == MIX_T and OUT_SUB % SEG_ROWS == 0
    slab = pl.BlockSpec((LRU_WIDTH // LANES, None, TILE_PITCH, LANES), lambda i: (0, i, 0, 0))
    return pl.pallas_call(
        kern,
        grid=(m // tm,),
        in_specs=[
            rows(D_MODEL), rows(ATTN_WIDTH), slab, slab,
            rows(3 * COL, COL_G_ATTN // 3), rows(3 * COL, COL_CONV_C // 3), rows(COL, COL_G_LRU),
            halo_before, halo_after,
            pl.BlockSpec((None, tm, PLE_DIM), lambda i: (l, i, 0)),
            _layer_resident(cw, l), _layer_resident(wo_b, l), _layer_resident(pn, l), _layer_resident(wg_b, l),
            _layer_resident(wp_b, l), _layer_resident(fn, 0),
        ],
        out_specs=rows(D_MODEL),
        out_shape=jax.ShapeDtypeStruct((m, D_MODEL), F32),
        scratch_shapes=[pltpu.VMEM((tm + 2 * SUBLANES, CONV_WIDTH), F32)],
        compiler_params=pltpu.CompilerParams(
            dimension_semantics=("arbitrary",), vmem_limit_bytes=VMEM_LIMIT),
        name="outple",
    )(h2d, ya, hf, hb, *([z2d] * 5), p3d, cw, wo_b, pn, wg_b, wp_b, fn)


def _trunk(x, p, prm, bias):
    b, s, _ = x.shape
    m = b * s
    depth = p.shape[0]
    h = x.reshape(m, D_MODEL)
    p3d = p.reshape(depth, m, PLE_DIM)
    for l in range(depth):
        z = _inproj(h, prm["norm_mix"], prm["w_in"], l)
        z3 = z.reshape(b, s, IN_WIDTH)
        ya = _attention(z3, bias, prm["attn_sink"], l)
        hf, hb = _mixers(z3, prm["lru_conv_w"], prm["lru_conv_b"],
                         prm["wai"], prm["lru_b_a"], prm["lru_b_i"], prm["lru_L"], l)
        h = _outple(h, ya.reshape(m, ATTN_WIDTH), hf, hb, z, p3d,
                    prm["conv_w"], prm["w_out"], prm["ple_norm"], prm["ple_w_gate"], prm["ple_w_proj"],
                    prm["final_norm"], l, s // OUT_TM, last=(l == depth - 1))
    return h.reshape(b, s, D_MODEL)


def kernel(x_prompt, x_sample, p_prompt, p_sample, norm_mix, w_in, w_out, rel_bias, attn_sink, conv_w,
           lru_conv_w, lru_conv_b, lru_w_a, lru_b_a, lru_w_i, lru_b_i, lru_L, ple_norm, ple_w_gate,
           ple_w_proj, final_norm):
    col_scale = jnp.where(jnp.arange(IN_WIDTH) < ATTN_WIDTH, Q_SCALE, 1.0).astype(F32)
    prm = dict(
        norm_mix=norm_mix[:, None, :], w_in=(w_in * col_scale).astype(BF16), w_out=w_out.astype(BF16),
        attn_sink=attn_sink, conv_w=conv_w, lru_conv_w=lru_conv_w, lru_conv_b=lru_conv_b,
        wai=jnp.concatenate([lru_w_a, lru_w_i], axis=-1).astype(BF16),
        lru_b_a=lru_b_a, lru_b_i=lru_b_i, lru_L=lru_L, ple_norm=ple_norm[:, None, :],
        ple_w_gate=ple_w_gate.astype(BF16), ple_w_proj=ple_w_proj.astype(BF16),
        final_norm=final_norm[None, None, :])
    bias = _rel_bias(rel_bias)
    y_prompt = _trunk(x_prompt, p_prompt, prm, bias)
    y_sample = _trunk(x_sample, p_sample, prm, bias)
    return (y_prompt, y_sample)
```
